```python
import math
import jax, jax.numpy as jnp
from jax import lax
import numpy as np

D_MODEL = 1024
BATCH = 8
SEQ = 2048
DEPTH = 4

N_MEM = 256
EPS = 1e-6
NEG_INF = -1e30
Q_BLOCK = 128

DIFF_HEADS = 8
DIFF_QK_DIM = 64
DIFF_V_DIM = 2 * DIFF_QK_DIM
DIFF_QK_WIDTH = 2 * DIFF_HEADS * DIFF_QK_DIM
DIFF_WIDTH = DIFF_HEADS * DIFF_V_DIM

DIL_GROUPS = ((128, 1), (512, 4), (2048, 16))
N_DIL_GROUPS = len(DIL_GROUPS)
DIL_HEADS = 4
DIL_HEAD_DIM = 128
DIL_WIDTH = DIL_HEADS * DIL_HEAD_DIM
DIL_QKV_WIDTH = N_DIL_GROUPS * DIL_WIDTH

MEM_HEADS = 4
MEM_HEAD_DIM = 128
MEM_WIDTH = MEM_HEADS * MEM_HEAD_DIM

N_BRANCH = 3

REL_BUCKETS = 32
REL_MAX_DIST = 1024
N_BIAS_HEADS = DIFF_HEADS + N_DIL_GROUPS * DIL_HEADS

IN_SIZES = (DIFF_QK_WIDTH, DIFF_QK_WIDTH, DIFF_WIDTH, DIFF_WIDTH,
            DIL_QKV_WIDTH, DIL_QKV_WIDTH, DIL_QKV_WIDTH, DIL_WIDTH,
            MEM_WIDTH, MEM_WIDTH, N_BRANCH * D_MODEL)
N_IN = sum(IN_SIZES)

kernel_name = 'hybrid_gated_diff_dilated_mem_encoder'


def rmsnorm(x, g=None):
    xf = x.astype(jnp.float32)
    y = xf * lax.rsqrt(jnp.mean(xf * xf, axis=-1, keepdims=True) + EPS)
    if g is not None:
        y = y * g.astype(jnp.float32)
    return y.astype(x.dtype)


def t5_bucket(rel):
    half = REL_BUCKETS // 2
    max_exact = half // 2
    ret = jnp.where(rel > 0, half, 0)
    n = jnp.abs(rel)
    nf = jnp.maximum(n, 1).astype(jnp.float32)
    large = max_exact + (jnp.log(nf / max_exact) / math.log(REL_MAX_DIST / max_exact)
                         * (half - max_exact)).astype(jnp.int32)
    large = jnp.minimum(large, half - 1)
    return ret + jnp.where(n < max_exact, n, large)


def diff_attention(q, k, v, lam, bias_tab):
    B, S = q.shape[0], q.shape[1]
    nq = S // Q_BLOCK
    scale = DIFF_QK_DIM ** -0.5
    qb = q.reshape(B, nq, Q_BLOCK, 2, DIFF_HEADS, DIFF_QK_DIM).transpose(1, 0, 2, 3, 4, 5)
    kpos = jnp.arange(S, dtype=jnp.int32)

    def one_block(args):
        qblk, i = args
        qpos = i * Q_BLOCK + jnp.arange(Q_BLOCK, dtype=jnp.int32)
        bias = bias_tab[t5_bucket(kpos[None, :] - qpos[:, None])]
        bias = jnp.transpose(bias, (2, 0, 1)).astype(jnp.float32)
        s = jnp.einsum('bqmhd,bkmhd->bmhqk', qblk, k,
                       preferred_element_type=jnp.float32) * scale + bias
        p = jax.nn.softmax(s, axis=-1)
        w = p[:, 0] - lam * p[:, 1]
        return jnp.einsum('bhqk,bkhe->bqhe', w.astype(v.dtype), v)

    out = lax.map(one_block, (qb, jnp.arange(nq, dtype=jnp.int32)))
    return out.transpose(1, 0, 2, 3, 4).reshape(B, S, DIFF_HEADS, DIFF_V_DIM)


def dilated_group(q, k, v, window, dilation, bias_tab):
    B, S, H, dh = q.shape
    r = dilation
    half = window // (2 * r)
    nb = half
    L = S // r
    nblk = -(-L // nb)
    Lp = nblk * nb
    Bn = B * r
    scale = dh ** -0.5

    def to_strided(t):
        return t.reshape(B, L, r, H, dh).transpose(0, 2, 1, 3, 4).reshape(Bn, L, H, dh)

    def band(t):
        tp = jnp.pad(t, ((0, 0), (nb, Lp - L + nb), (0, 0), (0, 0)))
        tp = tp.reshape(Bn, nblk + 2, nb, H, dh)
        return jnp.concatenate([tp[:, :-2], tp[:, 1:-1], tp[:, 2:]], axis=2)

    qs = jnp.pad(to_strided(q), ((0, 0), (0, Lp - L), (0, 0), (0, 0))).reshape(Bn, nblk, nb, H, dh)
    kb = band(to_strided(k))
    vb = band(to_strided(v))

    qi = jnp.arange(nb, dtype=jnp.int32)
    kj = jnp.arange(3 * nb, dtype=jnp.int32) - nb
    delta = kj[None, :] - qi[:, None]
    ksub = jnp.arange(nblk, dtype=jnp.int32)[:, None] * nb + kj[None, :]
    valid = (jnp.abs(delta) <= half)[None, :, :] & ((ksub >= 0) & (ksub < L))[:, None, :]
    bias = bias_tab[t5_bucket(delta * r)].astype(jnp.float32).transpose(2, 0, 1)

    s = jnp.einsum('bnqhd,bnkhd->bnhqk', qs, kb,
                   preferred_element_type=jnp.float32) * scale + bias
    s = jnp.where(valid[None, :, None], s, NEG_INF)
    m = jnp.max(s, axis=-1, keepdims=True)
    e = jnp.exp(s - m)
    den = jnp.sum(e, axis=-1, keepdims=True)
    o = jnp.einsum('bnhqk,bnkhd->bnqhd', (e / den).astype(v.dtype), vb)
    lse = (m + jnp.log(den))[..., 0]

    o = o.reshape(Bn, Lp, H, dh)[:, :L]
    lse = lse.transpose(0, 1, 3, 2).reshape(Bn, Lp, H)[:, :L]
    o = o.reshape(B, r, L, H, dh).transpose(0, 2, 1, 3, 4).reshape(B, S, H, dh)
    lse = lse.reshape(B, r, L, H).transpose(0, 2, 1, 3).reshape(B, S, H)
    return o, lse


def memory_attention(q, mk, mv):
    s = jnp.einsum('bshd,bmhd->bhsm', q, mk,
                   preferred_element_type=jnp.float32) * (MEM_HEAD_DIM ** -0.5)
    p = jax.nn.softmax(s, axis=-1)
    return jnp.einsum('bhsm,bmhd->bshd', p.astype(mv.dtype), mv)


def mixer_layer(x, mem, layer_idx, g_norm, w_in, lam_p, w_mem_kv, g_mem,
                w_br_diff, w_br_dil, w_br_mem, w_out, rel_bias):
    B, S, _ = x.shape
    h = rmsnorm(x, g_norm)
    z = jnp.einsum('bsd,dn->bsn', h, w_in)
    splits = np.cumsum(IN_SIZES)[:-1].tolist()
    dq, dk, dv, dg, lq, lk, lv, lg, mq, mg, mgate = jnp.split(z, splits, axis=-1)

    lam_init = 0.8 - 0.6 * math.exp(-0.3 * layer_idx)
    lp = lam_p.astype(jnp.float32)
    lam = jnp.exp(jnp.dot(lp[0], lp[1])) - jnp.exp(jnp.dot(lp[2], lp[3])) + lam_init
    o_a = diff_attention(dq.reshape(B, S, 2, DIFF_HEADS, DIFF_QK_DIM),
                         dk.reshape(B, S, 2, DIFF_HEADS, DIFF_QK_DIM),
                         dv.reshape(B, S, DIFF_HEADS, DIFF_V_DIM),
                         lam, rel_bias[:, :DIFF_HEADS])
    o_a = rmsnorm(o_a) * (1.0 - lam_init)
    y_a = jnp.einsum('bse,ed->bsd', o_a.reshape(B, S, DIFF_WIDTH) * jax.nn.silu(dg), w_br_diff)

    lq = lq.reshape(B, S, N_DIL_GROUPS, DIL_HEADS, DIL_HEAD_DIM)
    lk = lk.reshape(B, S, N_DIL_GROUPS, DIL_HEADS, DIL_HEAD_DIM)
    lv = lv.reshape(B, S, N_DIL_GROUPS, DIL_HEADS, DIL_HEAD_DIM)
    outs, lses = [], []
    for g, (window, dilation) in enumerate(DIL_GROUPS):
        c0 = DIFF_HEADS + g * DIL_HEADS
        o_g, lse_g = dilated_group(lq[:, :, g], lk[:, :, g], lv[:, :, g], window, dilation,
                                   rel_bias[:, c0:c0 + DIL_HEADS])
        outs.append(o_g.astype(jnp.float32))
        lses.append(lse_g)
    wts = jax.nn.softmax(jnp.stack(lses, axis=0), axis=0)
    o_b = jnp.einsum('gbsh,gbshe->bshe', wts, jnp.stack(outs, axis=0)).astype(x.dtype)
    y_b = jnp.einsum('bse,ed->bsd', o_b.reshape(B, S, DIL_WIDTH) * jax.nn.silu(lg), w_br_dil)

    mem_n = rmsnorm(mem, g_mem)
    kv = jnp.einsum('bmd,dn->bmn', mem_n, w_mem_kv).reshape(B, mem.shape[1], 2, MEM_HEADS, MEM_HEAD_DIM)
    o_m = memory_attention(mq.reshape(B, S, MEM_HEADS, MEM_HEAD_DIM), kv[:, :, 0], kv[:, :, 1])
    y_m = jnp.einsum('bse,ed->bsd', o_m.reshape(B, S, MEM_WIDTH) * jax.nn.silu(mg), w_br_mem)

    gates = jax.nn.sigmoid(mgate).reshape(B, S, N_BRANCH, D_MODEL)
    merged = gates[:, :, 0] * y_a + gates[:, :, 1] * y_b + gates[:, :, 2] * y_m
    return x + jnp.einsum('bsd,de->bse', merged, w_out)


def setup_inputs(seed: int = 0) -> dict:
    key = jax.random.key(seed)
    ks = jax.random.split(key, 14)
    f32 = jnp.float32
    nrm = jax.random.normal
    return {
        'x': nrm(ks[0], (BATCH, SEQ, D_MODEL), f32),
        'mem': nrm(ks[1], (BATCH, N_MEM, D_MODEL), f32),
        'g_norm': 1.0 + 0.02 * nrm(ks[2], (DEPTH, D_MODEL), f32),
        'w_in': nrm(ks[3], (DEPTH, D_MODEL, N_IN), f32) * D_MODEL ** -0.5,
        'diff_lambda': 0.1 * nrm(ks[4], (DEPTH, 4, DIFF_QK_DIM), f32),
        'w_mem_kv': nrm(ks[5], (DEPTH, D_MODEL, 2 * MEM_WIDTH), f32) * D_MODEL ** -0.5,
        'g_mem': 1.0 + 0.02 * nrm(ks[6], (DEPTH, D_MODEL), f32),
        'w_br_diff': nrm(ks[7], (DEPTH, DIFF_WIDTH, D_MODEL), f32) * DIFF_WIDTH ** -0.5,
        'w_br_dil': nrm(ks[8], (DEPTH, DIL_WIDTH, D_MODEL), f32) * DIL_WIDTH ** -0.5,
        'w_br_mem': nrm(ks[9], (DEPTH, MEM_WIDTH, D_MODEL), f32) * MEM_WIDTH ** -0.5,
        'w_out': nrm(ks[10], (DEPTH, D_MODEL, D_MODEL), f32) * D_MODEL ** -0.5,
        'rel_bias': 0.2 * nrm(ks[11], (REL_BUCKETS, N_BIAS_HEADS), f32),
        'g_final': 1.0 + 0.02 * nrm(ks[12], (D_MODEL,), f32),
    }


def reference(x, mem, g_norm, w_in, diff_lambda, w_mem_kv, g_mem,
              w_br_diff, w_br_dil, w_br_mem, w_out, rel_bias, g_final):
    for l in range(DEPTH):
        x = mixer_layer(x, mem, l, g_norm[l], w_in[l], diff_lambda[l], w_mem_kv[l], g_mem[l],
                        w_br_diff[l], w_br_dil[l], w_br_mem[l], w_out[l], rel_bias)
    return rmsnorm(x, g_final)
```

```python
import functools
import math

import jax
import jax.numpy as jnp
from jax import lax
from jax.experimental import pallas as pl
from jax.experimental.pallas import tpu as pltpu

D_MODEL = 1024
N_MEM = 256
EPS = 1e-6
NEG_INF = -1e30

DIFF_HEADS = 8
DIFF_QK_DIM = 64
DIFF_V_DIM = 128
DIL_GROUPS = ((128, 1), (512, 4), (2048, 16))
DIL_HEADS = 4
HEAD_DIM = 128
DIL_WIDTH = DIL_HEADS * HEAD_DIM
DIL_HALF = 64
MEM_HEADS = 4
MEM_WIDTH = MEM_HEADS * HEAD_DIM
REL_BUCKETS = 32
REL_MAX_DIST = 1024

OFF_DQ, OFF_DK, OFF_DV, OFF_DG = 0, 1024, 2048, 3072
OFF_LQ, OFF_LK, OFF_LV, OFF_LG = 4096, 5632, 7168, 8704
OFF_MQ, OFF_MG, OFF_MGATE = 9216, 9728, 10240

LOG2E = 1.4426950408889634
LN2 = 0.6931471805599453

LANES = 128
MXU_EDGE = 256
VMEM_LIMIT_BYTES = 56 * 1024 * 1024

DIFF_TQ = 512
DIFF_TK = MXU_EDGE
DIL_TQ = 128
FIN_T = 256
MM_TM = 2048
MM_SUB = 512


def _params(n_grid_dims):
    return pltpu.CompilerParams(dimension_semantics=("arbitrary",) * n_grid_dims,
                                vmem_limit_bytes=VMEM_LIMIT_BYTES)


def _dot_nt(a, b):
    return lax.dot_general(a, b, (((1,), (1,)), ((), ())), preferred_element_type=jnp.float32)


def _sigmoid(x):
    return 1.0 / (1.0 + jnp.exp(-x))


def _rms_kernel(x_ref, g_ref, o_ref):
    x = x_ref[...]
    ms = jnp.mean(x * x, axis=-1, keepdims=True)
    o_ref[...] = (x * lax.rsqrt(ms + EPS) * g_ref[...]).astype(o_ref.dtype)


def _rmsnorm(x2d, g, out_dtype, tm=512):
    m, d = x2d.shape
    tm = min(tm, m)
    return pl.pallas_call(
        _rms_kernel,
        grid=(m // tm,),
        in_specs=[pl.BlockSpec((tm, d), lambda i: (i, 0)),
                  pl.BlockSpec((1, d), lambda i: (0, 0))],
        out_specs=pl.BlockSpec((tm, d), lambda i: (i, 0)),
        out_shape=jax.ShapeDtypeStruct((m, d), out_dtype),
        compiler_params=_params(1),
        name="rmsnorm",
    )(x2d, g.reshape(1, d))


def _mm_kernel(a_ref, w_ref, o_ref):
    sub = min(MM_SUB, a_ref.shape[0])

    def body(s, carry):
        r0 = pl.multiple_of(s * sub, sub)
        acc = jnp.dot(a_ref[pl.ds(r0, sub), :], w_ref[...], preferred_element_type=jnp.float32)
        o_ref[pl.ds(r0, sub), :] = acc.astype(o_ref.dtype)
        return carry
    lax.fori_loop(0, a_ref.shape[0] // sub, body, 0)


def _matmul(a, w, out_dtype, tn):
    m, k = a.shape
    n = w.shape[1]
    tm = min(MM_TM, m)
    return pl.pallas_call(
        _mm_kernel,
        grid=(m // tm, n // tn),
        in_specs=[pl.BlockSpec((tm, k), lambda i, j: (i, 0)),
                  pl.BlockSpec((k, tn), lambda i, j: (0, j))],
        out_specs=pl.BlockSpec((tm, tn), lambda i, j: (i, j)),
        out_shape=jax.ShapeDtypeStruct((m, n), out_dtype),
        compiler_params=_params(2),
        name="in_proj",
    )(a, w)


def _t5_bucket(rel):
    half = REL_BUCKETS // 2
    max_exact = half // 2
    ret = jnp.where(rel > 0, half, 0)
    n = jnp.abs(rel)
    nf = jnp.maximum(n, 1).astype(jnp.float32)
    large = max_exact + (jnp.log(nf / max_exact) / math.log(REL_MAX_DIST / max_exact)
                         * (half - max_exact)).astype(jnp.int32)
    large = jnp.minimum(large, half - 1)
    return ret + jnp.where(n < max_exact, n, large)


def _diff_bias_tiles(rel_bias, seq):
    nd = seq // DIFF_TK - 1
    d = jnp.arange(2 * nd + 1, dtype=jnp.int32)[:, None, None] - nd
    r = jnp.arange(DIFF_TK, dtype=jnp.int32)[None, :, None]
    c = jnp.arange(DIFF_TK, dtype=jnp.int32)[None, None, :]
    tab = rel_bias[:, :DIFF_HEADS][_t5_bucket(d * DIFF_TK + c - r)]
    return jnp.transpose(tab, (3, 0, 1, 2)) * LOG2E


def _dil_bias_tiles(rel_bias, g, seq):
    _, r = DIL_GROUPS[g]
    length = seq // r
    width = min(2 * DIL_TQ, length)
    shifts = jnp.array([0, -DIL_HALF, -(width - DIL_TQ)], dtype=jnp.int32)
    rr = jnp.arange(DIL_TQ, dtype=jnp.int32)[None, :, None]
    cc = jnp.arange(width, dtype=jnp.int32)[None, None, :]
    delta = shifts[:, None, None] + cc - rr
    c0 = DIFF_HEADS + g * DIL_HEADS
    bias = rel_bias[:, c0:c0 + DIL_HEADS][_t5_bucket(delta * r)] * LOG2E
    bias = jnp.where((jnp.abs(delta) <= DIL_HALF)[..., None], bias, NEG_INF)
    return jnp.transpose(bias, (0, 3, 1, 2))


def _diff_kernel(q0_ref, q1_ref, k0_ref, k1_ref, v_ref, bias_ref, lam_ref, o_ref,
                 s_scr, e_scr, mx_scr, vext_scr, acc_scr, *, lam_init, nkc):
    i = pl.program_id(2)
    tq = q0_ref.shape[0]
    nrb = tq // DIFF_TK

    @pl.when(i == 0)
    def _():
        for a in range(2):
            vext_scr[a, :, 0:LANES] = v_ref[:, a * LANES:(a + 1) * LANES]
            vext_scr[a, :, LANES:2 * LANES] = jnp.ones((vext_scr.shape[1], LANES), vext_scr.dtype)

    lp = lam_ref[...]
    lam = (jnp.exp(jnp.sum(lp[0:1] * lp[1:2], axis=-1, keepdims=True))
           - jnp.exp(jnp.sum(lp[2:3] * lp[3:4], axis=-1, keepdims=True)) + lam_init)

    lane = lax.broadcasted_iota(jnp.int32, (tq, LANES), 1)
    q_refs = (q0_ref, q1_ref)
    k_refs = (k0_ref, k1_ref)

    for a in range(2):
        head_lanes = (lane >= a * DIFF_QK_DIM) & (lane < (a + 1) * DIFF_QK_DIM)
        for m in range(2):
            q = q_refs[m][...]
            qm = jnp.where(head_lanes, q, jnp.zeros_like(q))
            k_ref = k_refs[m]
            mx_scr[...] = jnp.full(mx_scr.shape, -jnp.inf, jnp.float32)

            def qk_body(j, carry, qm=qm, k_ref=k_ref, a=a):
                k0 = pl.multiple_of(j * DIFF_TK, DIFF_TK)
                s = _dot_nt(qm, k_ref[pl.ds(k0, DIFF_TK), :])
                for rb in range(nrb):
                    rows = slice(rb * DIFF_TK, (rb + 1) * DIFF_TK)
                    sb = s[rows] + bias_ref[a, j - nrb * i - rb + (nkc - 1)]
                    s_scr[j, rows, :] = sb
                    mx_scr[rows, :] = jnp.maximum(mx_scr[rows, :],
                                                  jnp.maximum(sb[:, :LANES], sb[:, LANES:]))
                return carry
            lax.fori_loop(0, nkc, qk_body, 0)

            row_max = jnp.max(mx_scr[...], axis=-1, keepdims=True)
            mb = jnp.broadcast_to(row_max, (tq, DIFF_TK))

            def exp_body(j, carry, mb=mb):
                e_scr[j] = jnp.exp2(s_scr[j] - mb).astype(e_scr.dtype)
                return carry
            lax.fori_loop(0, nkc, exp_body, 0)

            acc = jnp.dot(e_scr[0], vext_scr[a, 0:DIFF_TK, :], preferred_element_type=jnp.float32)
            for j in range(1, nkc):
                acc = acc + jnp.dot(e_scr[j], vext_scr[a, j * DIFF_TK:(j + 1) * DIFF_TK, :],
                                    preferred_element_type=jnp.float32)
            acc_scr[m] = acc

        n0 = acc_scr[0]
        n1 = acc_scr[1]
        out = n0[:, :LANES] / n0[:, LANES:] - lam * (n1[:, :LANES] / n1[:, LANES:])
        ms = jnp.mean(out * out, axis=-1, keepdims=True)
        o_ref[:, a * LANES:(a + 1) * LANES] = out * lax.rsqrt(ms + EPS) * (1.0 - lam_init)


def _diff_attention(zd, bias_tiles, lam_p, lam_init):
    b, s, _ = zd.shape
    tq = DIFF_TQ
    nkc = s // DIFF_TK
    kern = functools.partial(_diff_kernel, lam_init=lam_init, nkc=nkc)
    kblk = OFF_DK // LANES
    vblk = OFF_DV // (2 * LANES)
    return pl.pallas_call(
        kern,
        grid=(DIFF_HEADS // 2, b, s // tq),
        in_specs=[
            pl.BlockSpec((None, tq, LANES), lambda hp, bb, i: (bb, i, hp)),
            pl.BlockSpec((None, tq, LANES), lambda hp, bb, i: (bb, i, DIFF_HEADS // 2 + hp)),
            pl.BlockSpec((None, s, LANES), lambda hp, bb, i: (bb, 0, kblk + hp)),
            pl.BlockSpec((None, s, LANES), lambda hp, bb, i: (bb, 0, kblk + DIFF_HEADS // 2 + hp)),
            pl.BlockSpec((None, s, 2 * LANES), lambda hp, bb, i: (bb, 0, vblk + hp)),
            pl.BlockSpec((2, 2 * nkc - 1, DIFF_TK, DIFF_TK), lambda hp, bb, i: (hp, 0, 0, 0)),
            pl.BlockSpec((4, DIFF_QK_DIM), lambda hp, bb, i: (0, 0)),
        ],
        out_specs=pl.BlockSpec((None, tq, 2 * LANES), lambda hp, bb, i: (bb, i, hp)),
        out_shape=jax.ShapeDtypeStruct((b, s, DIFF_HEADS * DIFF_V_DIM), jnp.float32),
        scratch_shapes=[
            pltpu.VMEM((nkc, tq, DIFF_TK), jnp.float32),
            pltpu.VMEM((nkc, tq, DIFF_TK), jnp.bfloat16),
            pltpu.VMEM((tq, LANES), jnp.float32),
            pltpu.VMEM((2, s, 2 * LANES), jnp.bfloat16),
            pltpu.VMEM((2, tq, 2 * LANES), jnp.float32),
        ],
        compiler_params=_params(3),
        name="diff_attn",
    )(zd, zd, zd, zd, zd, bias_tiles, lam_p)


def _dil_kernel(z_ref, bias_ref, o_ref, lse_ref, *, r, length, width):
    nqb = length // DIL_TQ
    zc = 3 * DIL_WIDTH
    lane = lax.broadcasted_iota(jnp.int32, (DIL_TQ, LANES), 1)

    for c in range(r):
        def qb_body(qb, carry, c=c):
            q0 = pl.multiple_of(qb * DIL_TQ, DIL_TQ)
            ws = pl.multiple_of(jnp.clip(q0 - DIL_HALF, 0, length - width), DIL_HALF)
            var = jnp.where(qb == 0, 0, jnp.where(qb == nqb - 1, 2, 1))
            lse_tile = jnp.zeros((DIL_TQ, LANES), jnp.float32)
            for h in range(DIL_HEADS):
                col = c * zc + h * HEAD_DIM
                q = z_ref[pl.ds(q0, DIL_TQ), col:col + HEAD_DIM]
                kw = z_ref[pl.ds(ws, width), col + DIL_WIDTH:col + DIL_WIDTH + HEAD_DIM]
                vw = z_ref[pl.ds(ws, width), col + 2 * DIL_WIDTH:col + 2 * DIL_WIDTH + HEAD_DIM]
                sc = _dot_nt(q, kw) + bias_ref[var, h]
                mrow = jnp.max(sc, axis=-1, keepdims=True)
                e = jnp.exp2(sc - mrow)
                den = jnp.sum(e, axis=-1, keepdims=True)
                o = jnp.dot(e.astype(vw.dtype), vw, preferred_element_type=jnp.float32) / den
                oc = c * DIL_WIDTH + h * HEAD_DIM
                o_ref[pl.ds(q0, DIL_TQ), oc:oc + HEAD_DIM] = o
                lse_tile = jnp.where(lane == h, mrow + jnp.log2(den), lse_tile)
            lse_ref[pl.ds(q0, DIL_TQ), c * LANES:(c + 1) * LANES] = lse_tile
            return carry
        lax.fori_loop(0, nqb, qb_body, 0)


def _dil_attention(zl, bias_tiles, g):
    b, s, zc = zl.shape
    _, r = DIL_GROUPS[g]
    length = s // r
    width = bias_tiles.shape[-1]
    kern = functools.partial(_dil_kernel, r=r, length=length, width=width)
    o, lse = pl.pallas_call(
        kern,
        grid=(b,),
        in_specs=[pl.BlockSpec((None, length, r * zc), lambda bb: (bb, 0, 0)),
                  pl.BlockSpec(bias_tiles.shape, lambda bb: (0, 0, 0, 0))],
        out_specs=[pl.BlockSpec((None, length, r * DIL_WIDTH), lambda bb: (bb, 0, 0)),
                   pl.BlockSpec((None, length, r * LANES), lambda bb: (bb, 0, 0))],
        out_shape=[jax.ShapeDtypeStruct((b, length, r * DIL_WIDTH), jnp.float32),
                   jax.ShapeDtypeStruct((b, length, r * LANES), jnp.float32)],
        compiler_params=_params(1),
        name=f"dil_attn_g{g}",
    )(zl.reshape(b, length, r * zc), bias_tiles)
    return o.reshape(b, s, DIL_WIDTH), lse.reshape(b, s, LANES)


def _finish_kernel(x_ref, oa_ref, ob0_ref, ob1_ref, ob2_ref, l0_ref, l1_ref, l2_ref, mq_ref, zg_ref,
                   mkv_ref, wa_ref, wb_ref, wm_ref, wo_ref, g_ref, *out_refs, final):
    t = x_ref.shape[0]
    bf16 = jnp.bfloat16

    dg = zg_ref[:, 0:1024]
    ya = jnp.dot((oa_ref[...] * (dg * _sigmoid(dg))).astype(bf16), wa_ref[...],
                 preferred_element_type=jnp.float32)

    l0, l1, l2 = l0_ref[...], l1_ref[...], l2_ref[...]
    lmax = jnp.maximum(jnp.maximum(l0, l1), l2)
    w0, w1, w2 = jnp.exp2(l0 - lmax), jnp.exp2(l1 - lmax), jnp.exp2(l2 - lmax)
    inv = 1.0 / (w0 + w1 + w2)
    w0, w1, w2 = w0 * inv, w1 * inv, w2 * inv
    ob_refs = (ob0_ref, ob1_ref, ob2_ref)
    parts = []
    for h in range(DIL_HEADS):
        cols = slice(h * HEAD_DIM, (h + 1) * HEAD_DIM)
        acc = None
        for wg, ob_ref in zip((w0, w1, w2), ob_refs):
            term = wg[:, h:h + 1] * ob_ref[:, cols]
            acc = term if acc is None else acc + term
        parts.append(acc)
    ob = jnp.concatenate(parts, axis=-1)
    lg = zg_ref[:, 1024:1536]
    yb = jnp.dot((ob * (lg * _sigmoid(lg))).astype(bf16), wb_ref[...], preferred_element_type=jnp.float32)

    parts = []
    for h in range(MEM_HEADS):
        cols = slice(h * HEAD_DIM, (h + 1) * HEAD_DIM)
        sc = _dot_nt(mq_ref[:, cols], mkv_ref[:, cols])
        e = jnp.exp2(sc - jnp.max(sc, axis=-1, keepdims=True))
        den = jnp.sum(e, axis=-1, keepdims=True)
        vcols = slice(MEM_WIDTH + h * HEAD_DIM, MEM_WIDTH + (h + 1) * HEAD_DIM)
        parts.append(jnp.dot(e.astype(bf16), mkv_ref[:, vcols], preferred_element_type=jnp.float32) / den)
    om = jnp.concatenate(parts, axis=-1)
    mg = zg_ref[:, 1536:2048]
    ym = jnp.dot((om * (mg * _sigmoid(mg))).astype(bf16), wm_ref[...], preferred_element_type=jnp.float32)

    merged = (_sigmoid(zg_ref[:, 2048:3072]) * ya + _sigmoid(zg_ref[:, 3072:4096]) * yb
              + _sigmoid(zg_ref[:, 4096:5120]) * ym)
    xn = x_ref[...] + jnp.dot(merged.astype(bf16), wo_ref[...], preferred_element_type=jnp.float32)
    hn = xn * lax.rsqrt(jnp.mean(xn * xn, axis=-1, keepdims=True) + EPS) * g_ref[...]
    if final:
        out_refs[0][...] = hn
    else:
        out_refs[0][...] = xn
        out_refs[1][...] = hn.astype(out_refs[1].dtype)


def _finish(x, oa, obs, lses, zd, zg, mkv, wa, wb, wm, wo, g_next, final):
    b, s, d = x.shape
    t = FIN_T
    row = lambda width: pl.BlockSpec((None, t, width), lambda bb, i: (bb, i, 0))
    full = lambda arr: pl.BlockSpec(arr.shape, lambda bb, i: (0,) * arr.ndim)
    in_specs = [row(d), row(d), row(DIL_WIDTH), row(DIL_WIDTH), row(DIL_WIDTH),
                row(LANES), row(LANES), row(LANES),
                pl.BlockSpec((None, t, MEM_WIDTH), lambda bb, i: (bb, i, 3072 // MEM_WIDTH)),
                row(zg.shape[-1]),
                pl.BlockSpec((None, N_MEM, 2 * MEM_WIDTH), lambda bb, i: (bb, 0, 0)),
                full(wa), full(wb), full(wm), full(wo),
                pl.BlockSpec((1, d), lambda bb, i: (0, 0))]
    if final:
        out_specs = [row(d)]
        out_shape = [jax.ShapeDtypeStruct((b, s, d), jnp.float32)]
    else:
        out_specs = [row(d), row(d)]
        out_shape = [jax.ShapeDtypeStruct((b, s, d), jnp.float32),
                     jax.ShapeDtypeStruct((b, s, d), jnp.bfloat16)]
    return pl.pallas_call(
        functools.partial(_finish_kernel, final=final),
        grid=(b, s // t),
        in_specs=in_specs,
        out_specs=out_specs,
        out_shape=out_shape,
        compiler_params=_params(2),
        name="finish",
    )(x, oa, *obs, *lses, zd, zg, mkv, wa, wb, wm, wo, g_next.reshape(1, d))


def kernel(x, mem, g_norm, w_in, diff_lambda, w_mem_kv, g_mem, w_br_diff, w_br_dil, w_br_mem, w_out,
           rel_bias, g_final):
    b, s, d = x.shape
    depth = w_in.shape[0]
    bf16 = jnp.bfloat16
    m_rows = b * s

    diff_tiles = _diff_bias_tiles(rel_bias, s)
    dil_tiles = [_dil_bias_tiles(rel_bias, g, s) for g in range(len(DIL_GROUPS))]
    mem2d = mem.reshape(b * N_MEM, d)

    c_diff = DIFF_QK_DIM ** -0.5 * LOG2E
    c_head = HEAD_DIM ** -0.5 * LOG2E

    h = _rmsnorm(x.reshape(m_rows, d), g_norm[0], bf16)
    out = None
    for l in range(depth):
        w = w_in[l]
        w_d = jnp.concatenate([w[:, OFF_DQ:OFF_DK] * c_diff, w[:, OFF_DK:OFF_DG],
                               w[:, OFF_MQ:OFF_MG] * c_head], axis=1).astype(bf16)
        w_l = [jnp.concatenate([w[:, OFF_LQ + g * DIL_WIDTH:OFF_LQ + (g + 1) * DIL_WIDTH] * c_head,
                                w[:, OFF_LK + g * DIL_WIDTH:OFF_LK + (g + 1) * DIL_WIDTH],
                                w[:, OFF_LV + g * DIL_WIDTH:OFF_LV + (g + 1) * DIL_WIDTH]],
                               axis=1).astype(bf16) for g in range(len(DIL_GROUPS))]
        w_g = jnp.concatenate([w[:, OFF_DG:OFF_LQ], w[:, OFF_LG:OFF_MQ], w[:, OFF_MG:]],
                              axis=1).astype(bf16)

        zd = _matmul(h, w_d, bf16, tn=1792).reshape(b, s, -1)
        zls = [_matmul(h, w_l[g], bf16, tn=1536).reshape(b, s, -1) for g in range(len(DIL_GROUPS))]
        zg = _matmul(h, w_g, jnp.float32, tn=1024).reshape(b, s, -1)

        mem_n = _rmsnorm(mem2d, g_mem[l], bf16)
        mkv = _matmul(mem_n, w_mem_kv[l].astype(bf16), bf16, tn=1024).reshape(b, N_MEM, 2 * MEM_WIDTH)

        lam_init = 0.8 - 0.6 * math.exp(-0.3 * l)
        oa = _diff_attention(zd, diff_tiles, diff_lambda[l], lam_init)
        obs, lses = zip(*[_dil_attention(zls[g], dil_tiles[g], g) for g in range(len(DIL_GROUPS))])

        final = l == depth - 1
        g_next = g_final if final else g_norm[l + 1]
        res = _finish(x, oa, obs, lses, zd, zg, mkv, w_br_diff[l].astype(bf16), w_br_dil[l].astype(bf16),
                      w_br_mem[l].astype(bf16), w_out[l].astype(bf16), g_next, final)
        if final:
            out = res[0]
        else:
            x, h3 = res
            h = h3.reshape(m_rows, d)
    return out
```

```python
import functools
import math

import jax
import jax.numpy as jnp
from jax import lax
from jax.experimental import pallas as pl
from jax.experimental.pallas import tpu as pltpu

D_MODEL = 1024
N_MEM = 256
EPS = 1e-6
NEG_INF = -1e30

DIFF_HEADS = 8
DIFF_QK_DIM = 64
DIFF_V_DIM = 128
DIL_GROUPS = ((128, 1), (512, 4), (2048, 16))
DIL_HEADS = 4
HEAD_DIM = 128
DIL_WIDTH = DIL_HEADS * HEAD_DIM
DIL_HALF = 64
MEM_HEADS = 4
MEM_WIDTH = MEM_HEADS * HEAD_DIM
REL_BUCKETS = 32
REL_MAX_DIST = 1024

OFF_DQ, OFF_DK, OFF_DV, OFF_DG = 0, 1024, 2048, 3072
OFF_LQ, OFF_LK, OFF_LV, OFF_LG = 4096, 5632, 7168, 8704
OFF_MQ, OFF_MG, OFF_MGATE = 9216, 9728, 10240

LOG2E = 1.4426950408889634
LN2 = 0.6931471805599453

LANES = 128
MXU_EDGE = 256
VMEM_LIMIT_BYTES = 56 * 1024 * 1024

DIFF_TQ = 512
DIFF_TK = MXU_EDGE
DIL_TQ = 128
FIN_T = 256
MM_TM = 2048
MM_SUB = 512


def _params(n_grid_dims):
    return pltpu.CompilerParams(dimension_semantics=("arbitrary",) * n_grid_dims,
                                vmem_limit_bytes=VMEM_LIMIT_BYTES)


def _dot_nt(a, b):
    return lax.dot_general(a, b, (((1,), (1,)), ((), ())), preferred_element_type=jnp.float32)


def _sigmoid(x):
    return 1.0 / (1.0 + jnp.exp(-x))


def _rms_kernel(x_ref, g_ref, o_ref):
    x = x_ref[...]
    ms = jnp.mean(x * x, axis=-1, keepdims=True)
    o_ref[...] = (x * lax.rsqrt(ms + EPS) * g_ref[...]).astype(o_ref.dtype)


def _rmsnorm(x2d, g, out_dtype, tm=512):
    m, d = x2d.shape
    tm = min(tm, m)
    return pl.pallas_call(
        _rms_kernel,
        grid=(m // tm,),
        in_specs=[pl.BlockSpec((tm, d), lambda i: (i, 0)),
                  pl.BlockSpec((1, d), lambda i: (0, 0))],
        out_specs=pl.BlockSpec((tm, d), lambda i: (i, 0)),
        out_shape=jax.ShapeDtypeStruct((m, d), out_dtype),
        compiler_params=_params(1),
        name="rmsnorm",
    )(x2d, g.reshape(1, d))


def _mm_kernel(a_ref, w_ref, o_ref):
    sub = min(MM_SUB, a_ref.shape[0])

    def body(s, carry):
        r0 = pl.multiple_of(s * sub, sub)
        acc = jnp.dot(a_ref[pl.ds(r0, sub), :], w_ref[...], preferred_element_type=jnp.float32)
        o_ref[pl.ds(r0, sub), :] = acc.astype(o_ref.dtype)
        return carry
    lax.fori_loop(0, a_ref.shape[0] // sub, body, 0)


def _matmul(a, w, out_dtype, tn):
    m, k = a.shape
    n = w.shape[1]
    tm = min(MM_TM, m)
    return pl.pallas_call(
        _mm_kernel,
        grid=(m // tm, n // tn),
        in_specs=[pl.BlockSpec((tm, k), lambda i, j: (i, 0)),
                  pl.BlockSpec((k, tn), lambda i, j: (0, j))],
        out_specs=pl.BlockSpec((tm, tn), lambda i, j: (i, j)),
        out_shape=jax.ShapeDtypeStruct((m, n), out_dtype),
        compiler_params=_params(2),
        name="in_proj",
    )(a, w)


def _t5_bucket(rel):
    half = REL_BUCKETS // 2
    max_exact = half // 2
    ret = jnp.where(rel > 0, half, 0)
    n = jnp.abs(rel)
    nf = jnp.maximum(n, 1).astype(jnp.float32)
    large = max_exact + (jnp.log(nf / max_exact) / math.log(REL_MAX_DIST / max_exact)
                         * (half - max_exact)).astype(jnp.int32)
    large = jnp.minimum(large, half - 1)
    return ret + jnp.where(n < max_exact, n, large)


def _toeplitz(vals, rows, cols):
    lead = vals.shape[:-1]
    p = rows + cols
    u = jnp.concatenate([vals[..., rows - 1:], jnp.zeros(lead + (1,), vals.dtype), vals[..., :rows - 1]],
                        axis=-1)
    flat = jnp.broadcast_to(u[..., None, :], lead + (rows, p)).reshape(lead + (rows * p,))
    return flat[..., :rows * (p - 1)].reshape(lead + (rows, p - 1))[..., :cols]


def _diff_bias_tiles(rel_bias, seq):
    nd = seq // DIFF_TK - 1
    rel = jnp.arange(2 * seq - 1, dtype=jnp.int32) - (seq - 1)
    tvec = jnp.take(rel_bias[:, :DIFF_HEADS], _t5_bucket(rel), axis=0).T * LOG2E
    band = _toeplitz(tvec, DIFF_TK, (2 * nd + 1) * DIFF_TK)
    return jnp.transpose(band.reshape(DIFF_HEADS, DIFF_TK, 2 * nd + 1, DIFF_TK), (0, 2, 1, 3))


def _dil_bias_tiles(rel_bias, g, seq):
    _, r = DIL_GROUPS[g]
    length = seq // r
    width = min(2 * DIL_TQ, length)
    shifts = jnp.array([0, -DIL_HALF, -(width - DIL_TQ)], dtype=jnp.int32)
    delta = shifts[:, None] + jnp.arange(DIL_TQ + width - 1, dtype=jnp.int32)[None, :] - (DIL_TQ - 1)
    c0 = DIFF_HEADS + g * DIL_HEADS
    bias = jnp.take(rel_bias[:, c0:c0 + DIL_HEADS], _t5_bucket(delta * r), axis=0) * LOG2E
    bias = jnp.where((jnp.abs(delta) <= DIL_HALF)[..., None], bias, NEG_INF)
    return _toeplitz(jnp.transpose(bias, (0, 2, 1)), DIL_TQ, width)


def _diff_kernel(q0_ref, q1_ref, k0_ref, k1_ref, v_ref, bias_ref, lam_ref, o_ref,
                 s_scr, mx_scr, vext_scr, acc_scr, *, lam_init, nkc):
    i = pl.program_id(2)
    tq = q0_ref.shape[0]
    nrb = tq // DIFF_TK

    @pl.when(i == 0)
    def _():
        for a in range(2):
            vext_scr[a, :, 0:LANES] = v_ref[:, a * LANES:(a + 1) * LANES]
            vext_scr[a, :, LANES:2 * LANES] = jnp.ones((vext_scr.shape[1], LANES), vext_scr.dtype)

    lp = lam_ref[...]
    lam = (jnp.exp(jnp.sum(lp[0:1] * lp[1:2], axis=-1, keepdims=True))
           - jnp.exp(jnp.sum(lp[2:3] * lp[3:4], axis=-1, keepdims=True)) + lam_init)

    lane = lax.broadcasted_iota(jnp.int32, (tq, LANES), 1)
    q_refs = (q0_ref, q1_ref)
    k_refs = (k0_ref, k1_ref)

    for a in range(2):
        head_lanes = (lane >= a * DIFF_QK_DIM) & (lane < (a + 1) * DIFF_QK_DIM)
        for m in range(2):
            q = q_refs[m][...]
            qm = jnp.where(head_lanes, q, jnp.zeros_like(q))
            k_ref = k_refs[m]
            mx_scr[...] = jnp.full(mx_scr.shape, -jnp.inf, jnp.float32)

            for j in range(nkc):
                s = _dot_nt(qm, k_ref[j * DIFF_TK:(j + 1) * DIFF_TK, :])
                for rb in range(nrb):
                    rows = slice(rb * DIFF_TK, (rb + 1) * DIFF_TK)
                    sb = s[rows] + bias_ref[a, j - nrb * i - rb + (nkc - 1)]
                    s_scr[j, rows, :] = sb
                    mx_scr[rows, :] = jnp.maximum(mx_scr[rows, :],
                                                  jnp.maximum(sb[:, :LANES], sb[:, LANES:]))

            row_max = jnp.max(mx_scr[...], axis=-1, keepdims=True)
            mb = jnp.broadcast_to(row_max, (tq, DIFF_TK))

            acc = None
            for j in range(nkc):
                e = jnp.exp2(s_scr[j] - mb).astype(vext_scr.dtype)
                part = jnp.dot(e, vext_scr[a, j * DIFF_TK:(j + 1) * DIFF_TK, :],
                               preferred_element_type=jnp.float32)
                acc = part if acc is None else acc + part
            acc_scr[m] = acc

        n0 = acc_scr[0]
        n1 = acc_scr[1]
        out = n0[:, :LANES] / n0[:, LANES:] - lam * (n1[:, :LANES] / n1[:, LANES:])
        ms = jnp.mean(out * out, axis=-1, keepdims=True)
        o_ref[:, a * LANES:(a + 1) * LANES] = out * lax.rsqrt(ms + EPS) * (1.0 - lam_init)


def _diff_attention(zd, bias_tiles, lam_p, lam_init):
    b, s, _ = zd.shape
    tq = DIFF_TQ
    nkc = s // DIFF_TK
    kern = functools.partial(_diff_kernel, lam_init=lam_init, nkc=nkc)
    kblk = OFF_DK // LANES
    vblk = OFF_DV // (2 * LANES)
    return pl.pallas_call(
        kern,
        grid=(DIFF_HEADS // 2, b, s // tq),
        in_specs=[
            pl.BlockSpec((None, tq, LANES), lambda hp, bb, i: (bb, i, hp)),
            pl.BlockSpec((None, tq, LANES), lambda hp, bb, i: (bb, i, DIFF_HEADS // 2 + hp)),
            pl.BlockSpec((None, s, LANES), lambda hp, bb, i: (bb, 0, kblk + hp)),
            pl.BlockSpec((None, s, LANES), lambda hp, bb, i: (bb, 0, kblk + DIFF_HEADS // 2 + hp)),
            pl.BlockSpec((None, s, 2 * LANES), lambda hp, bb, i: (bb, 0, vblk + hp)),
            pl.BlockSpec((2, 2 * nkc - 1, DIFF_TK, DIFF_TK), lambda hp, bb, i: (hp, 0, 0, 0)),
            pl.BlockSpec((4, DIFF_QK_DIM), lambda hp, bb, i: (0, 0)),
        ],
        out_specs=pl.BlockSpec((None, tq, 2 * LANES), lambda hp, bb, i: (bb, i, hp)),
        out_shape=jax.ShapeDtypeStruct((b, s, DIFF_HEADS * DIFF_V_DIM), jnp.float32),
        scratch_shapes=[
            pltpu.VMEM((nkc, tq, DIFF_TK), jnp.float32),
            pltpu.VMEM((tq, LANES), jnp.float32),
            pltpu.VMEM((2, s, 2 * LANES), jnp.bfloat16),
            pltpu.VMEM((2, tq, 2 * LANES), jnp.float32),
        ],
        compiler_params=_params(3),
        name="diff_attn",
    )(zd, zd, zd, zd, zd, bias_tiles, lam_p)


def _dil_kernel(z_ref, bias_ref, o_ref, lse_ref, *, r, length, width):
    nqb = length // DIL_TQ
    zc = 3 * DIL_WIDTH
    lane = lax.broadcasted_iota(jnp.int32, (DIL_TQ, LANES), 1)

    for c in range(r):
        def qb_body(qb, carry, c=c):
            q0 = pl.multiple_of(qb * DIL_TQ, DIL_TQ)
            ws = pl.multiple_of(jnp.clip(q0 - DIL_HALF, 0, length - width), DIL_HALF)
            var = jnp.where(qb == 0, 0, jnp.where(qb == nqb - 1, 2, 1))
            lse_tile = jnp.zeros((DIL_TQ, LANES), jnp.float32)
            for h in range(DIL_HEADS):
                col = c * zc + h * HEAD_DIM
                q = z_ref[pl.ds(q0, DIL_TQ), col:col + HEAD_DIM]
                kw = z_ref[pl.ds(ws, width), col + DIL_WIDTH:col + DIL_WIDTH + HEAD_DIM]
                vw = z_ref[pl.ds(ws, width), col + 2 * DIL_WIDTH:col + 2 * DIL_WIDTH + HEAD_DIM]
                sc = _dot_nt(q, kw) + bias_ref[var, h]
                mrow = jnp.max(sc, axis=-1, keepdims=True)
                e = jnp.exp2(sc - mrow)
                den = jnp.sum(e, axis=-1, keepdims=True)
                o = jnp.dot(e.astype(vw.dtype), vw, preferred_element_type=jnp.float32) / den
                oc = c * DIL_WIDTH + h * HEAD_DIM
                o_ref[pl.ds(q0, DIL_TQ), oc:oc + HEAD_DIM] = o
                lse_tile = jnp.where(lane == h, mrow + jnp.log2(den), lse_tile)
            lse_ref[pl.ds(q0, DIL_TQ), c * LANES:(c + 1) * LANES] = lse_tile
            return carry
        lax.fori_loop(0, nqb, qb_body, 0)


def _dil_attention(zl, bias_tiles, g):
    b, s, zc = zl.shape
    _, r = DIL_GROUPS[g]
    length = s // r
    width = bias_tiles.shape[-1]
    kern = functools.partial(_dil_kernel, r=r, length=length, width=width)
    o, lse = pl.pallas_call(
        kern,
        grid=(b,),
        in_specs=[pl.BlockSpec((None, length, r * zc), lambda bb: (bb, 0, 0)),
                  pl.BlockSpec(bias_tiles.shape, lambda bb: (0, 0, 0, 0))],
        out_specs=[pl.BlockSpec((None, length, r * DIL_WIDTH), lambda bb: (bb, 0, 0)),
                   pl.BlockSpec((None, length, r * LANES), lambda bb: (bb, 0, 0))],
        out_shape=[jax.ShapeDtypeStruct((b, length, r * DIL_WIDTH), jnp.float32),
                   jax.ShapeDtypeStruct((b, length, r * LANES), jnp.float32)],
        compiler_params=_params(1),
        name=f"dil_attn_g{g}",
    )(zl.reshape(b, length, r * zc), bias_tiles)
    return o.reshape(b, s, DIL_WIDTH), lse.reshape(b, s, LANES)


def _finish_kernel(x_ref, oa_ref, ob0_ref, ob1_ref, ob2_ref, l0_ref, l1_ref, l2_ref, mq_ref, zg_ref,
                   mkv_ref, wa_ref, wb_ref, wm_ref, wo_ref, g_ref, *out_refs, final):
    t = x_ref.shape[0]
    bf16 = jnp.bfloat16

    dg = zg_ref[:, 0:1024]
    ya = jnp.dot((oa_ref[...] * (dg * _sigmoid(dg))).astype(bf16), wa_ref[...],
                 preferred_element_type=jnp.float32)

    l0, l1, l2 = l0_ref[...], l1_ref[...], l2_ref[...]
    lmax = jnp.maximum(jnp.maximum(l0, l1), l2)
    w0, w1, w2 = jnp.exp2(l0 - lmax), jnp.exp2(l1 - lmax), jnp.exp2(l2 - lmax)
    inv = 1.0 / (w0 + w1 + w2)
    w0, w1, w2 = w0 * inv, w1 * inv, w2 * inv
    ob_refs = (ob0_ref, ob1_ref, ob2_ref)
    parts = []
    for h in range(DIL_HEADS):
        cols = slice(h * HEAD_DIM, (h + 1) * HEAD_DIM)
        acc = None
        for wg, ob_ref in zip((w0, w1, w2), ob_refs):
            term = wg[:, h:h + 1] * ob_ref[:, cols]
            acc = term if acc is None else acc + term
        parts.append(acc)
    ob = jnp.concatenate(parts, axis=-1)
    lg = zg_ref[:, 1024:1536]
    yb = jnp.dot((ob * (lg * _sigmoid(lg))).astype(bf16), wb_ref[...], preferred_element_type=jnp.float32)

    parts = []
    for h in range(MEM_HEADS):
        cols = slice(h * HEAD_DIM, (h + 1) * HEAD_DIM)
        sc = _dot_nt(mq_ref[:, cols], mkv_ref[:, cols])
        e = jnp.exp2(sc - jnp.max(sc, axis=-1, keepdims=True))
        den = jnp.sum(e, axis=-1, keepdims=True)
        vcols = slice(MEM_WIDTH + h * HEAD_DIM, MEM_WIDTH + (h + 1) * HEAD_DIM)
        parts.append(jnp.dot(e.astype(bf16), mkv_ref[:, vcols], preferred_element_type=jnp.float32) / den)
    om = jnp.concatenate(parts, axis=-1)
    mg = zg_ref[:, 1536:2048]
    ym = jnp.dot((om * (mg * _sigmoid(mg))).astype(bf16), wm_ref[...], preferred_element_type=jnp.float32)

    merged = (_sigmoid(zg_ref[:, 2048:3072]) * ya + _sigmoid(zg_ref[:, 3072:4096]) * yb
              + _sigmoid(zg_ref[:, 4096:5120]) * ym)
    xn = x_ref[...] + jnp.dot(merged.astype(bf16), wo_ref[...], preferred_element_type=jnp.float32)
    hn = xn * lax.rsqrt(jnp.mean(xn * xn, axis=-1, keepdims=True) + EPS) * g_ref[...]
    if final:
        out_refs[0][...] = hn
    else:
        out_refs[0][...] = xn
        out_refs[1][...] = hn.astype(out_refs[1].dtype)


def _finish(x, oa, obs, lses, zd, zg, mkv, wa, wb, wm, wo, g_next, final):
    b, s, d = x.shape
    t = FIN_T
    row = lambda width: pl.BlockSpec((None, t, width), lambda bb, i: (bb, i, 0))
    full = lambda arr: pl.BlockSpec(arr.shape, lambda bb, i: (0,) * arr.ndim)
    in_specs = [row(d), row(d), row(DIL_WIDTH), row(DIL_WIDTH), row(DIL_WIDTH),
                row(LANES), row(LANES), row(LANES),
                pl.BlockSpec((None, t, MEM_WIDTH), lambda bb, i: (bb, i, 3072 // MEM_WIDTH)),
                row(zg.shape[-1]),
                pl.BlockSpec((None, N_MEM, 2 * MEM_WIDTH), lambda bb, i: (bb, 0, 0)),
                full(wa), full(wb), full(wm), full(wo),
                pl.BlockSpec((1, d), lambda bb, i: (0, 0))]
    if final:
        out_specs = [row(d)]
        out_shape = [jax.ShapeDtypeStruct((b, s, d), jnp.float32)]
    else:
        out_specs = [row(d), row(d)]
        out_shape = [jax.ShapeDtypeStruct((b, s, d), jnp.float32),
                     jax.ShapeDtypeStruct((b, s, d), jnp.bfloat16)]
    return pl.pallas_call(
        functools.partial(_finish_kernel, final=final),
        grid=(b, s // t),
        in_specs=in_specs,
        out_specs=out_specs,
        out_shape=out_shape,
        compiler_params=_params(2),
        name="finish",
    )(x, oa, *obs, *lses, zd, zg, mkv, wa, wb, wm, wo, g_next.reshape(1, d))


def kernel(x, mem, g_norm, w_in, diff_lambda, w_mem_kv, g_mem, w_br_diff, w_br_dil, w_br_mem, w_out,
           rel_bias, g_final):
    b, s, d = x.shape
    depth = w_in.shape[0]
    bf16 = jnp.bfloat16
    m_rows = b * s

    diff_tiles = _diff_bias_tiles(rel_bias, s)
    dil_tiles = [_dil_bias_tiles(rel_bias, g, s) for g in range(len(DIL_GROUPS))]
    mem2d = mem.reshape(b * N_MEM, d)

    c_diff = DIFF_QK_DIM ** -0.5 * LOG2E
    c_head = HEAD_DIM ** -0.5 * LOG2E

    h = _rmsnorm(x.reshape(m_rows, d), g_norm[0], bf16)
    out = None
    for l in range(depth):
        w = w_in[l]
        w_d = jnp.concatenate([w[:, OFF_DQ:OFF_DK] * c_diff, w[:, OFF_DK:OFF_DG],
                               w[:, OFF_MQ:OFF_MG] * c_head], axis=1).astype(bf16)
        w_l = [jnp.concatenate([w[:, OFF_LQ + g * DIL_WIDTH:OFF_LQ + (g + 1) * DIL_WIDTH] * c_head,
                                w[:, OFF_LK + g * DIL_WIDTH:OFF_LK + (g + 1) * DIL_WIDTH],
                                w[:, OFF_LV + g * DIL_WIDTH:OFF_LV + (g + 1) * DIL_WIDTH]],
                               axis=1).astype(bf16) for g in range(len(DIL_GROUPS))]
        w_g = jnp.concatenate([w[:, OFF_DG:OFF_LQ], w[:, OFF_LG:OFF_MQ], w[:, OFF_MG:]],
                              axis=1).astype(bf16)

        zd = _matmul(h, w_d, bf16, tn=1792).reshape(b, s, -1)
        zls = [_matmul(h, w_l[g], bf16, tn=1536).reshape(b, s, -1) for g in range(len(DIL_GROUPS))]
        zg = _matmul(h, w_g, jnp.float32, tn=1024).reshape(b, s, -1)

        mem_n = _rmsnorm(mem2d, g_mem[l], bf16)
        mkv = _matmul(mem_n, w_mem_kv[l].astype(bf16), bf16, tn=1024).reshape(b, N_MEM, 2 * MEM_WIDTH)

        lam_init = 0.8 - 0.6 * math.exp(-0.3 * l)
        oa = _diff_attention(zd, diff_tiles, diff_lambda[l], lam_init)
        obs, lses = zip(*[_dil_attention(zls[g], dil_tiles[g], g) for g in range(len(DIL_GROUPS))])

        final = l == depth - 1
        g_next = g_final if final else g_norm[l + 1]
        res = _finish(x, oa, obs, lses, zd, zg, mkv, w_br_diff[l].astype(bf16), w_br_dil[l].astype(bf16),
                      w_br_mem[l].astype(bf16), w_out[l].astype(bf16), g_next, final)
        if final:
            out = res[0]
        else:
            x, h3 = res
            h = h3.reshape(m_rows, d)
    return out
```

```python
import functools
import math

import jax
import jax.numpy as jnp
from jax import lax
from jax.experimental import pallas as pl
from jax.experimental.pallas import tpu as pltpu

D_MODEL = 1024
N_MEM = 256
EPS = 1e-6
NEG_INF = -1e30

DIFF_HEADS = 8
DIFF_QK_DIM = 64
DIFF_V_DIM = 128
DIL_GROUPS = ((128, 1), (512, 4), (2048, 16))
DIL_HEADS = 4
HEAD_DIM = 128
DIL_WIDTH = DIL_HEADS * HEAD_DIM
DIL_HALF = 64
MEM_HEADS = 4
MEM_WIDTH = MEM_HEADS * HEAD_DIM
REL_BUCKETS = 32
REL_MAX_DIST = 1024

OFF_DQ, OFF_DK, OFF_DV, OFF_DG = 0, 1024, 2048, 3072
OFF_LQ, OFF_LK, OFF_LV, OFF_LG = 4096, 5632, 7168, 8704
OFF_MQ, OFF_MG, OFF_MGATE = 9216, 9728, 10240

LOG2E = 1.4426950408889634
LN2 = 0.6931471805599453

LANES = 128
MXU_EDGE = 256
VMEM_LIMIT_BYTES = 56 * 1024 * 1024

DIFF_TQ = 512
DIFF_TK = MXU_EDGE
DIL_TQ = 128
DIL_UNROLL = 4
FIN_T = 256
MM_TM = 2048
MM_SUB = 512


def _params(n_grid_dims):
    return pltpu.CompilerParams(dimension_semantics=("arbitrary",) * n_grid_dims,
                                vmem_limit_bytes=VMEM_LIMIT_BYTES)


def _dot_nt(a, b):
    return lax.dot_general(a, b, (((1,), (1,)), ((), ())), preferred_element_type=jnp.float32)


def _sigmoid(x):
    return 1.0 / (1.0 + jnp.exp(-x))


def _rms_kernel(x_ref, g_ref, o_ref):
    x = x_ref[...]
    ms = jnp.mean(x * x, axis=-1, keepdims=True)
    o_ref[...] = (x * lax.rsqrt(ms + EPS) * g_ref[...]).astype(o_ref.dtype)


def _rmsnorm(x2d, g, out_dtype, tm=512):
    m, d = x2d.shape
    tm = min(tm, m)
    return pl.pallas_call(
        _rms_kernel,
        grid=(m // tm,),
        in_specs=[pl.BlockSpec((tm, d), lambda i: (i, 0)),
                  pl.BlockSpec((1, d), lambda i: (0, 0))],
        out_specs=pl.BlockSpec((tm, d), lambda i: (i, 0)),
        out_shape=jax.ShapeDtypeStruct((m, d), out_dtype),
        compiler_params=_params(1),
        name="rmsnorm",
    )(x2d, g.reshape(1, d))


def _mm_kernel(a_ref, w_ref, o_ref):
    sub = min(MM_SUB, a_ref.shape[0])

    def body(s, carry):
        r0 = pl.multiple_of(s * sub, sub)
        acc = jnp.dot(a_ref[pl.ds(r0, sub), :], w_ref[...], preferred_element_type=jnp.float32)
        o_ref[pl.ds(r0, sub), :] = acc.astype(o_ref.dtype)
        return carry
    lax.fori_loop(0, a_ref.shape[0] // sub, body, 0)


def _matmul(a, w, out_dtype, tn):
    m, k = a.shape
    n = w.shape[1]
    tm = min(MM_TM, m)
    return pl.pallas_call(
        _mm_kernel,
        grid=(m // tm, n // tn),
        in_specs=[pl.BlockSpec((tm, k), lambda i, j: (i, 0)),
                  pl.BlockSpec((k, tn), lambda i, j: (0, j))],
        out_specs=pl.BlockSpec((tm, tn), lambda i, j: (i, j)),
        out_shape=jax.ShapeDtypeStruct((m, n), out_dtype),
        compiler_params=_params(2),
        name="in_proj",
    )(a, w)


def _mm_strided_kernel(a_ref, w_ref, o_ref, scr, *, r):
    rows = a_ref.shape[0]
    length = rows // r
    n_slabs = w_ref.shape[1] // LANES
    sub = min(MM_SUB, rows)

    def body(s, carry):
        r0 = pl.multiple_of(s * sub, sub)
        acc = jnp.dot(a_ref[pl.ds(r0, sub), :], w_ref[...], preferred_element_type=jnp.float32)
        for k in range(n_slabs):
            scr[k, pl.ds(r0, sub), :] = acc[:, k * LANES:(k + 1) * LANES]
        return carry
    lax.fori_loop(0, rows // sub, body, 0)

    for c in range(r):
        for k in range(n_slabs):
            col = (c * n_slabs + k) * LANES
            o_ref[:, col:col + LANES] = scr[k, pl.ds(c, length, stride=r), :].astype(o_ref.dtype)


def _matmul_strided(a, w, batch, r, out_dtype):
    m, k = a.shape
    n = w.shape[1]
    seq = m // batch
    if r == 1:
        return _matmul(a, w, out_dtype, tn=n).reshape(batch, seq, n)
    return pl.pallas_call(
        functools.partial(_mm_strided_kernel, r=r),
        grid=(batch,),
        in_specs=[pl.BlockSpec((seq, k), lambda i: (i, 0)),
                  pl.BlockSpec((k, n), lambda i: (0, 0))],
        out_specs=pl.BlockSpec((None, seq // r, r * n), lambda i: (i, 0, 0)),
        out_shape=jax.ShapeDtypeStruct((batch, seq // r, r * n), out_dtype),
        scratch_shapes=[pltpu.VMEM((n // LANES, seq, LANES), jnp.float32)],
        compiler_params=_params(1),
        name=f"in_proj_stride{r}",
    )(a, w)


def _t5_bucket(rel):
    half = REL_BUCKETS // 2
    max_exact = half // 2
    ret = jnp.where(rel > 0, half, 0)
    n = jnp.abs(rel)
    nf = jnp.maximum(n, 1).astype(jnp.float32)
    large = max_exact + (jnp.log(nf / max_exact) / math.log(REL_MAX_DIST / max_exact)
                         * (half - max_exact)).astype(jnp.int32)
    large = jnp.minimum(large, half - 1)
    return ret + jnp.where(n < max_exact, n, large)


def _toeplitz_kernel(u_ref, o_ref):
    n_tiles, rows, tile_w = o_ref.shape
    x = jnp.broadcast_to(u_ref[...], (rows, u_ref.shape[-1]))
    y = pltpu.roll(x, 0, 1, stride=1, stride_axis=0)
    for d in range(n_tiles):
        o_ref[d] = y[:, d * tile_w:(d + 1) * tile_w]


def _toeplitz_tiles(vals, rows, n_tiles, tile_w):
    groups = vals.shape[0]
    cols = n_tiles * tile_w
    period = pl.next_power_of_2(rows + cols - 1)
    pad = jnp.zeros((groups, period - (rows + cols - 1)), vals.dtype)
    u = jnp.concatenate([vals[:, rows - 1:], pad, vals[:, :rows - 1]], axis=1).reshape(groups, 1, period)
    return pl.pallas_call(
        _toeplitz_kernel,
        grid=(groups,),
        in_specs=[pl.BlockSpec((None, 1, period), lambda g: (g, 0, 0))],
        out_specs=pl.BlockSpec((None, n_tiles, rows, tile_w), lambda g: (g, 0, 0, 0)),
        out_shape=jax.ShapeDtypeStruct((groups, n_tiles, rows, tile_w), vals.dtype),
        compiler_params=_params(1),
        name="toeplitz_tiles",
    )(u)


def _diff_bias_tiles(rel_bias, seq):
    nd = seq // DIFF_TK - 1
    rel = jnp.arange(2 * seq - 1, dtype=jnp.int32) - (seq - 1)
    tvec = jnp.take(rel_bias[:, :DIFF_HEADS], _t5_bucket(rel), axis=0).T * LOG2E
    return _toeplitz_tiles(tvec, DIFF_TK, 2 * nd + 1, DIFF_TK)


def _dil_bias_tiles(rel_bias, g, seq):
    _, r = DIL_GROUPS[g]
    length = seq // r
    width = min(2 * DIL_TQ, length)
    shifts = jnp.array([0, -DIL_HALF, -(width - DIL_TQ)], dtype=jnp.int32)
    delta = shifts[:, None] + jnp.arange(DIL_TQ + width - 1, dtype=jnp.int32)[None, :] - (DIL_TQ - 1)
    c0 = DIFF_HEADS + g * DIL_HEADS
    bias = jnp.take(rel_bias[:, c0:c0 + DIL_HEADS], _t5_bucket(delta * r), axis=0) * LOG2E
    bias = jnp.where((jnp.abs(delta) <= DIL_HALF)[..., None], bias, NEG_INF)
    vals = jnp.transpose(bias, (0, 2, 1)).reshape(3 * DIL_HEADS, DIL_TQ + width - 1)
    return _toeplitz_tiles(vals, DIL_TQ, 1, width).reshape(3, DIL_HEADS, DIL_TQ, width)


def _diff_kernel(q0_ref, q1_ref, k0_ref, k1_ref, v_ref, bias_ref, lam_ref, o_ref,
                 s_scr, mx_scr, vext_scr, acc_scr, *, lam_init, nkc):
    i = pl.program_id(2)
    tq = q0_ref.shape[0]
    nrb = tq // DIFF_TK

    @pl.when(i == 0)
    def _():
        for a in range(2):
            vext_scr[a, :, 0:LANES] = v_ref[:, a * LANES:(a + 1) * LANES]
            vext_scr[a, :, LANES:2 * LANES] = jnp.ones((vext_scr.shape[1], LANES), vext_scr.dtype)

    lp = lam_ref[...]
    lam = (jnp.exp(jnp.sum(lp[0:1] * lp[1:2], axis=-1, keepdims=True))
           - jnp.exp(jnp.sum(lp[2:3] * lp[3:4], axis=-1, keepdims=True)) + lam_init)

    lane = lax.broadcasted_iota(jnp.int32, (tq, LANES), 1)
    q_refs = (q0_ref, q1_ref)
    k_refs = (k0_ref, k1_ref)

    for a in range(2):
        head_lanes = (lane >= a * DIFF_QK_DIM) & (lane < (a + 1) * DIFF_QK_DIM)
        for m in range(2):
            u = 2 * a + m
            q = q_refs[m][...]
            qm = jnp.where(head_lanes, q, jnp.zeros_like(q))
            k_ref = k_refs[m]
            mx_scr[u] = jnp.full(mx_scr.shape[1:], -jnp.inf, jnp.float32)

            for j in range(nkc):
                s = _dot_nt(qm, k_ref[j * DIFF_TK:(j + 1) * DIFF_TK, :])
                for rb in range(nrb):
                    rows = slice(rb * DIFF_TK, (rb + 1) * DIFF_TK)
                    sb = s[rows] + bias_ref[a, j - nrb * i - rb + (nkc - 1)]
                    s_scr[u, j, rows, :] = sb
                    mx_scr[u, rows, :] = jnp.maximum(mx_scr[u, rows, :],
                                                     jnp.maximum(sb[:, :LANES], sb[:, LANES:]))

            row_max = jnp.max(mx_scr[u], axis=-1, keepdims=True)
            mb = jnp.broadcast_to(row_max, (tq, DIFF_TK))

            acc = None
            for j in range(nkc):
                e = jnp.exp2(s_scr[u, j] - mb).astype(vext_scr.dtype)
                part = jnp.dot(e, vext_scr[a, j * DIFF_TK:(j + 1) * DIFF_TK, :],
                               preferred_element_type=jnp.float32)
                acc = part if acc is None else acc + part
            acc_scr[u] = acc

        n0 = acc_scr[2 * a]
        n1 = acc_scr[2 * a + 1]
        out = n0[:, :LANES] / n0[:, LANES:] - lam * (n1[:, :LANES] / n1[:, LANES:])
        ms = jnp.mean(out * out, axis=-1, keepdims=True)
        o_ref[:, a * LANES:(a + 1) * LANES] = (out * lax.rsqrt(ms + EPS) * (1.0 - lam_init)).astype(o_ref.dtype)


def _diff_attention(zd, bias_tiles, lam_p, lam_init):
    b, s, _ = zd.shape
    tq = DIFF_TQ
    nkc = s // DIFF_TK
    kern = functools.partial(_diff_kernel, lam_init=lam_init, nkc=nkc)
    kblk = OFF_DK // LANES
    vblk = OFF_DV // (2 * LANES)
    return pl.pallas_call(
        kern,
        grid=(DIFF_HEADS // 2, b, s // tq),
        in_specs=[
            pl.BlockSpec((None, tq, LANES), lambda hp, bb, i: (bb, i, hp)),
            pl.BlockSpec((None, tq, LANES), lambda hp, bb, i: (bb, i, DIFF_HEADS // 2 + hp)),
            pl.BlockSpec((None, s, LANES), lambda hp, bb, i: (bb, 0, kblk + hp)),
            pl.BlockSpec((None, s, LANES), lambda hp, bb, i: (bb, 0, kblk + DIFF_HEADS // 2 + hp)),
            pl.BlockSpec((None, s, 2 * LANES), lambda hp, bb, i: (bb, 0, vblk + hp)),
            pl.BlockSpec((2, 2 * nkc - 1, DIFF_TK, DIFF_TK), lambda hp, bb, i: (hp, 0, 0, 0)),
            pl.BlockSpec((4, DIFF_QK_DIM), lambda hp, bb, i: (0, 0)),
        ],
        out_specs=pl.BlockSpec((None, tq, 2 * LANES), lambda hp, bb, i: (bb, i, hp)),
        out_shape=jax.ShapeDtypeStruct((b, s, DIFF_HEADS * DIFF_V_DIM), jnp.bfloat16),
        scratch_shapes=[
            pltpu.VMEM((4, nkc, tq, DIFF_TK), jnp.float32),
            pltpu.VMEM((4, tq, LANES), jnp.float32),
            pltpu.VMEM((2, s, 2 * LANES), jnp.bfloat16),
            pltpu.VMEM((4, tq, 2 * LANES), jnp.float32),
        ],
        compiler_params=_params(3),
        name="diff_attn",
    )(zd, zd, zd, zd, zd, bias_tiles, lam_p)


def _dil_kernel(z_ref, bias_ref, o_ref, lse_ref, *, r, length, width):
    nqb = length // DIL_TQ
    zc = 3 * DIL_WIDTH
    lane = lax.broadcasted_iota(jnp.int32, (DIL_TQ, LANES), 1)

    for c in range(r):
        def qb_body(qb, carry, c=c):
            q0 = pl.multiple_of(qb * DIL_TQ, DIL_TQ)
            ws = pl.multiple_of(jnp.clip(q0 - DIL_HALF, 0, length - width), DIL_HALF)
            var = jnp.where(qb == 0, 0, jnp.where(qb == nqb - 1, 2, 1))
            out_rows = pl.ds(q0 * r + c, DIL_TQ, stride=r) if r > 1 else pl.ds(q0, DIL_TQ)
            lse_tile = jnp.zeros((DIL_TQ, LANES), jnp.float32)
            for h in range(DIL_HEADS):
                col = c * zc + h * HEAD_DIM
                q = z_ref[pl.ds(q0, DIL_TQ), col:col + HEAD_DIM]
                kw = z_ref[pl.ds(ws, width), col + DIL_WIDTH:col + DIL_WIDTH + HEAD_DIM]
                vw = z_ref[pl.ds(ws, width), col + 2 * DIL_WIDTH:col + 2 * DIL_WIDTH + HEAD_DIM]
                sc = _dot_nt(q, kw) + bias_ref[var, h]
                mrow = jnp.max(sc, axis=-1, keepdims=True)
                e = jnp.exp2(sc - mrow)
                den = jnp.sum(e, axis=-1, keepdims=True)
                o_ref[h, out_rows, :] = jnp.dot(e.astype(vw.dtype), vw,
                                                preferred_element_type=jnp.float32) / den
                lse_tile = jnp.where(lane == h, mrow + jnp.log2(den), lse_tile)
            lse_ref[out_rows, :] = lse_tile
            return carry
        lax.fori_loop(0, nqb, qb_body, 0, unroll=min(nqb, DIL_UNROLL))


def _dil_attention(zl_view, bias_tiles, g):
    b, length, _ = zl_view.shape
    _, r = DIL_GROUPS[g]
    s = length * r
    width = bias_tiles.shape[-1]
    kern = functools.partial(_dil_kernel, r=r, length=length, width=width)
    return pl.pallas_call(
        kern,
        grid=(b,),
        in_specs=[pl.BlockSpec((None, length, zl_view.shape[-1]), lambda bb: (bb, 0, 0)),
                  pl.BlockSpec(bias_tiles.shape, lambda bb: (0, 0, 0, 0))],
        out_specs=[pl.BlockSpec((None, DIL_HEADS, s, HEAD_DIM), lambda bb: (bb, 0, 0, 0)),
                   pl.BlockSpec((None, s, LANES), lambda bb: (bb, 0, 0))],
        out_shape=[jax.ShapeDtypeStruct((b, DIL_HEADS, s, HEAD_DIM), jnp.float32),
                   jax.ShapeDtypeStruct((b, s, LANES), jnp.float32)],
        compiler_params=_params(1),
        name=f"dil_attn_g{g}",
    )(zl_view, bias_tiles)


def _finish_kernel(x_ref, oa_ref, ob0_ref, ob1_ref, ob2_ref, l0_ref, l1_ref, l2_ref, mq_ref, zg_ref,
                   mkv_ref, wa_ref, wb_ref, wm_ref, wo_ref, g_ref, *out_refs, final):
    bf16 = jnp.bfloat16
    f32 = jnp.float32

    dg = zg_ref[:, 0:1024].astype(f32)
    ya = jnp.dot((oa_ref[...].astype(f32) * (dg * _sigmoid(dg))).astype(bf16), wa_ref[...],
                 preferred_element_type=jnp.float32)

    l0, l1, l2 = l0_ref[...], l1_ref[...], l2_ref[...]
    lmax = jnp.maximum(jnp.maximum(l0, l1), l2)
    w0, w1, w2 = jnp.exp2(l0 - lmax), jnp.exp2(l1 - lmax), jnp.exp2(l2 - lmax)
    inv = 1.0 / (w0 + w1 + w2)
    w0, w1, w2 = w0 * inv, w1 * inv, w2 * inv
    ob_refs = (ob0_ref, ob1_ref, ob2_ref)
    parts = []
    for h in range(DIL_HEADS):
        acc = None
        for wg, ob_ref in zip((w0, w1, w2), ob_refs):
            term = wg[:, h:h + 1] * ob_ref[h]
            acc = term if acc is None else acc + term
        parts.append(acc)
    ob = jnp.concatenate(parts, axis=-1)
    lg = zg_ref[:, 1024:1536].astype(f32)
    yb = jnp.dot((ob * (lg * _sigmoid(lg))).astype(bf16), wb_ref[...], preferred_element_type=jnp.float32)

    parts = []
    for h in range(MEM_HEADS):
        cols = slice(h * HEAD_DIM, (h + 1) * HEAD_DIM)
        sc = _dot_nt(mq_ref[:, cols], mkv_ref[:, cols])
        e = jnp.exp2(sc - jnp.max(sc, axis=-1, keepdims=True))
        den = jnp.sum(e, axis=-1, keepdims=True)
        vcols = slice(MEM_WIDTH + h * HEAD_DIM, MEM_WIDTH + (h + 1) * HEAD_DIM)
        parts.append(jnp.dot(e.astype(bf16), mkv_ref[:, vcols], preferred_element_type=jnp.float32) / den)
    om = jnp.concatenate(parts, axis=-1)
    mg = zg_ref[:, 1536:2048].astype(f32)
    ym = jnp.dot((om * (mg * _sigmoid(mg))).astype(bf16), wm_ref[...], preferred_element_type=jnp.float32)

    merged = (_sigmoid(zg_ref[:, 2048:3072].astype(f32)) * ya + _sigmoid(zg_ref[:, 3072:4096].astype(f32)) * yb
              + _sigmoid(zg_ref[:, 4096:5120].astype(f32)) * ym)
    xn = x_ref[...] + jnp.dot(merged.astype(bf16), wo_ref[...], preferred_element_type=jnp.float32)
    hn = xn * lax.rsqrt(jnp.mean(xn * xn, axis=-1, keepdims=True) + EPS) * g_ref[...]
    if final:
        out_refs[0][...] = hn
    else:
        out_refs[0][...] = xn
        out_refs[1][...] = hn.astype(out_refs[1].dtype)


def _finish(x, oa, obs, lses, zd, zg, mkv, wa, wb, wm, wo, g_next, final):
    b, s, d = x.shape
    t = FIN_T
    row = lambda width: pl.BlockSpec((None, t, width), lambda bb, i: (bb, i, 0))
    full = lambda arr: pl.BlockSpec(arr.shape, lambda bb, i: (0,) * arr.ndim)
    heads = pl.BlockSpec((None, DIL_HEADS, t, HEAD_DIM), lambda bb, i: (bb, 0, i, 0))
    in_specs = [row(d), row(d), heads, heads, heads,
                row(LANES), row(LANES), row(LANES),
                pl.BlockSpec((None, t, MEM_WIDTH), lambda bb, i: (bb, i, 3072 // MEM_WIDTH)),
                row(zg.shape[-1]),
                pl.BlockSpec((None, N_MEM, 2 * MEM_WIDTH), lambda bb, i: (bb, 0, 0)),
                full(wa), full(wb), full(wm), full(wo),
                pl.BlockSpec((1, d), lambda bb, i: (0, 0))]
    if final:
        out_specs = [row(d)]
        out_shape = [jax.ShapeDtypeStruct((b, s, d), jnp.float32)]
    else:
        out_specs = [row(d), row(d)]
        out_shape = [jax.ShapeDtypeStruct((b, s, d), jnp.float32),
                     jax.ShapeDtypeStruct((b, s, d), jnp.bfloat16)]
    return pl.pallas_call(
        functools.partial(_finish_kernel, final=final),
        grid=(b, s // t),
        in_specs=in_specs,
        out_specs=out_specs,
        out_shape=out_shape,
        compiler_params=_params(2),
        name="finish",
    )(x, oa, *obs, *lses, zd, zg, mkv, wa, wb, wm, wo, g_next.reshape(1, d))


def kernel(x, mem, g_norm, w_in, diff_lambda, w_mem_kv, g_mem, w_br_diff, w_br_dil, w_br_mem, w_out,
           rel_bias, g_final):
    b, s, d = x.shape
    depth = w_in.shape[0]
    bf16 = jnp.bfloat16
    m_rows = b * s

    diff_tiles = _diff_bias_tiles(rel_bias, s)
    dil_tiles = [_dil_bias_tiles(rel_bias, g, s) for g in range(len(DIL_GROUPS))]
    mem2d = mem.reshape(b * N_MEM, d)

    c_diff = DIFF_QK_DIM ** -0.5 * LOG2E
    c_head = HEAD_DIM ** -0.5 * LOG2E

    h = _rmsnorm(x.reshape(m_rows, d), g_norm[0], bf16)
    out = None
    for l in range(depth):
        w = w_in[l]
        w_d = jnp.concatenate([w[:, OFF_DQ:OFF_DK] * c_diff, w[:, OFF_DK:OFF_DG],
                               w[:, OFF_MQ:OFF_MG] * c_head], axis=1).astype(bf16)
        w_l = [jnp.concatenate([w[:, OFF_LQ + g * DIL_WIDTH:OFF_LQ + (g + 1) * DIL_WIDTH] * c_head,
                                w[:, OFF_LK + g * DIL_WIDTH:OFF_LK + (g + 1) * DIL_WIDTH],
                                w[:, OFF_LV + g * DIL_WIDTH:OFF_LV + (g + 1) * DIL_WIDTH]],
                               axis=1).astype(bf16) for g in range(len(DIL_GROUPS))]
        w_g = jnp.concatenate([w[:, OFF_DG:OFF_LQ], w[:, OFF_LG:OFF_MQ], w[:, OFF_MG:]],
                              axis=1).astype(bf16)

        zd = _matmul(h, w_d, bf16, tn=1792).reshape(b, s, -1)
        zls = [_matmul_strided(h, w_l[g], b, DIL_GROUPS[g][1], bf16) for g in range(len(DIL_GROUPS))]
        zg = _matmul(h, w_g, bf16, tn=1024).reshape(b, s, -1)

        mem_n = _rmsnorm(mem2d, g_mem[l], bf16)
        mkv = _matmul(mem_n, w_mem_kv[l].astype(bf16), bf16, tn=1024).reshape(b, N_MEM, 2 * MEM_WIDTH)

        lam_init = 0.8 - 0.6 * math.exp(-0.3 * l)
        oa = _diff_attention(zd, diff_tiles, diff_lambda[l], lam_init)
        obs, lses = zip(*[_dil_attention(zls[g], dil_tiles[g], g) for g in range(len(DIL_GROUPS))])

        final = l == depth - 1
        g_next = g_final if final else g_norm[l + 1]
        res = _finish(x, oa, obs, lses, zd, zg, mkv, w_br_diff[l].astype(bf16), w_br_dil[l].astype(bf16),
                      w_br_mem[l].astype(bf16), w_out[l].astype(bf16), g_next, final)
        if final:
            out = res[0]
        else:
            x, h3 = res
            h = h3.reshape(m_rows, d)
    return out
```

```python
import functools
import math

import jax
import jax.numpy as jnp
from jax import lax
from jax.experimental import pallas as pl
from jax.experimental.pallas import tpu as pltpu

D_MODEL = 1024
N_MEM = 256
EPS = 1e-6
NEG_INF = -1e30

DIFF_HEADS = 8
DIFF_QK_DIM = 64
DIFF_V_DIM = 128
DIL_GROUPS = ((128, 1), (512, 4), (2048, 16))
DIL_HEADS = 4
HEAD_DIM = 128
DIL_WIDTH = DIL_HEADS * HEAD_DIM
DIL_HALF = 64
MEM_HEADS = 4
MEM_WIDTH = MEM_HEADS * HEAD_DIM
REL_BUCKETS = 32
REL_MAX_DIST = 1024

OFF_DQ, OFF_DK, OFF_DV, OFF_DG = 0, 1024, 2048, 3072
OFF_LQ, OFF_LK, OFF_LV, OFF_LG = 4096, 5632, 7168, 8704
OFF_MQ, OFF_MG, OFF_MGATE = 9216, 9728, 10240

LOG2E = 1.4426950408889634
LN2 = 0.6931471805599453

LANES = 128
MXU_EDGE = 256
VMEM_LIMIT_BYTES = 56 * 1024 * 1024

DIFF_TQ = 512
DIFF_TK = MXU_EDGE
DIL_TQ = 128
DIL_UNROLL = 4
FIN_T = 256
MM_TM = 2048
MM_SUB = 512


def _params(n_grid_dims):
    return pltpu.CompilerParams(dimension_semantics=("arbitrary",) * n_grid_dims,
                                vmem_limit_bytes=VMEM_LIMIT_BYTES)


def _dot_nt(a, b):
    return lax.dot_general(a, b, (((1,), (1,)), ((), ())), preferred_element_type=jnp.float32)


def _rms_kernel(x_ref, g_ref, o_ref):
    x = x_ref[...]
    ms = jnp.mean(x * x, axis=-1, keepdims=True)
    o_ref[...] = (x * lax.rsqrt(ms + EPS) * g_ref[...]).astype(o_ref.dtype)


def _rmsnorm(x2d, g, out_dtype, tm=512):
    m, d = x2d.shape
    tm = min(tm, m)
    return pl.pallas_call(
        _rms_kernel,
        grid=(m // tm,),
        in_specs=[pl.BlockSpec((tm, d), lambda i: (i, 0)),
                  pl.BlockSpec((1, d), lambda i: (0, 0))],
        out_specs=pl.BlockSpec((tm, d), lambda i: (i, 0)),
        out_shape=jax.ShapeDtypeStruct((m, d), out_dtype),
        compiler_params=_params(1),
        name="rmsnorm",
    )(x2d, g.reshape(1, d))


def _mm_kernel(a_ref, w_ref, o_ref):
    sub = min(MM_SUB, a_ref.shape[0])

    def body(s, carry):
        r0 = pl.multiple_of(s * sub, sub)
        acc = jnp.dot(a_ref[pl.ds(r0, sub), :], w_ref[...], preferred_element_type=jnp.float32)
        o_ref[pl.ds(r0, sub), :] = acc.astype(o_ref.dtype)
        return carry
    lax.fori_loop(0, a_ref.shape[0] // sub, body, 0)


def _matmul(a, w, out_dtype, tn):
    m, k = a.shape
    n = w.shape[1]
    tm = min(MM_TM, m)
    return pl.pallas_call(
        _mm_kernel,
        grid=(m // tm, n // tn),
        in_specs=[pl.BlockSpec((tm, k), lambda i, j: (i, 0)),
                  pl.BlockSpec((k, tn), lambda i, j: (0, j))],
        out_specs=pl.BlockSpec((tm, tn), lambda i, j: (i, j)),
        out_shape=jax.ShapeDtypeStruct((m, n), out_dtype),
        compiler_params=_params(2),
        name="in_proj",
    )(a, w)


def _mm_strided_kernel(a_ref, w_ref, o_ref, scr, *, r):
    rows = a_ref.shape[0]
    length = rows // r
    n_slabs = w_ref.shape[1] // LANES
    sub = min(MM_SUB, rows)

    def body(s, carry):
        r0 = pl.multiple_of(s * sub, sub)
        acc = jnp.dot(a_ref[pl.ds(r0, sub), :], w_ref[...], preferred_element_type=jnp.float32)
        for k in range(n_slabs):
            scr[k, pl.ds(r0, sub), :] = acc[:, k * LANES:(k + 1) * LANES]
        return carry
    lax.fori_loop(0, rows // sub, body, 0)

    for c in range(r):
        for k in range(n_slabs):
            o_ref[c, :, k * LANES:(k + 1) * LANES] = scr[k, pl.ds(c, length, stride=r), :].astype(o_ref.dtype)


def _matmul_strided(a, w, batch, r, out_dtype):
    m, k = a.shape
    n = w.shape[1]
    seq = m // batch
    if r == 1:
        return _matmul(a, w, out_dtype, tn=n).reshape(batch, 1, seq, n)
    return pl.pallas_call(
        functools.partial(_mm_strided_kernel, r=r),
        grid=(batch,),
        in_specs=[pl.BlockSpec((seq, k), lambda i: (i, 0)),
                  pl.BlockSpec((k, n), lambda i: (0, 0))],
        out_specs=pl.BlockSpec((None, r, seq // r, n), lambda i: (i, 0, 0, 0)),
        out_shape=jax.ShapeDtypeStruct((batch, r, seq // r, n), out_dtype),
        scratch_shapes=[pltpu.VMEM((n // LANES, seq, LANES), jnp.float32)],
        compiler_params=_params(1),
        name=f"in_proj_stride{r}",
    )(a, w)


def _t5_bucket(rel):
    half = REL_BUCKETS // 2
    max_exact = half // 2
    ret = jnp.where(rel > 0, half, 0)
    n = jnp.abs(rel)
    nf = jnp.maximum(n, 1).astype(jnp.float32)
    large = max_exact + (jnp.log(nf / max_exact) / math.log(REL_MAX_DIST / max_exact)
                         * (half - max_exact)).astype(jnp.int32)
    large = jnp.minimum(large, half - 1)
    return ret + jnp.where(n < max_exact, n, large)


def _toeplitz_kernel(u_ref, o_ref):
    n_tiles, rows, tile_w = o_ref.shape
    x = jnp.broadcast_to(u_ref[...], (rows, u_ref.shape[-1]))
    y = pltpu.roll(x, 0, 1, stride=1, stride_axis=0)
    for d in range(n_tiles):
        o_ref[d] = y[:, d * tile_w:(d + 1) * tile_w]


def _toeplitz_tiles(vals, rows, n_tiles, tile_w):
    groups = vals.shape[0]
    cols = n_tiles * tile_w
    period = pl.next_power_of_2(rows + cols - 1)
    pad = jnp.zeros((groups, period - (rows + cols - 1)), vals.dtype)
    u = jnp.concatenate([vals[:, rows - 1:], pad, vals[:, :rows - 1]], axis=1).reshape(groups, 1, period)
    return pl.pallas_call(
        _toeplitz_kernel,
        grid=(groups,),
        in_specs=[pl.BlockSpec((None, 1, period), lambda g: (g, 0, 0))],
        out_specs=pl.BlockSpec((None, n_tiles, rows, tile_w), lambda g: (g, 0, 0, 0)),
        out_shape=jax.ShapeDtypeStruct((groups, n_tiles, rows, tile_w), vals.dtype),
        compiler_params=_params(1),
        name="toeplitz_tiles",
    )(u)


def _diff_bias_tiles(rel_bias, seq):
    nd = seq // DIFF_TK - 1
    rel = jnp.arange(2 * seq - 1, dtype=jnp.int32) - (seq - 1)
    tvec = jnp.take(rel_bias[:, :DIFF_HEADS], _t5_bucket(rel), axis=0).T * LOG2E
    return _toeplitz_tiles(tvec, DIFF_TK, 2 * nd + 1, DIFF_TK)


def _dil_bias_tiles(rel_bias, g, seq):
    _, r = DIL_GROUPS[g]
    length = seq // r
    width = min(2 * DIL_TQ, length)
    shifts = jnp.array([0, -DIL_HALF, -(width - DIL_TQ)], dtype=jnp.int32)
    delta = shifts[:, None] + jnp.arange(DIL_TQ + width - 1, dtype=jnp.int32)[None, :] - (DIL_TQ - 1)
    c0 = DIFF_HEADS + g * DIL_HEADS
    bias = jnp.take(rel_bias[:, c0:c0 + DIL_HEADS], _t5_bucket(delta * r), axis=0) * LOG2E
    bias = jnp.where((jnp.abs(delta) <= DIL_HALF)[..., None], bias, NEG_INF)
    vals = jnp.transpose(bias, (0, 2, 1)).reshape(3 * DIL_HEADS, DIL_TQ + width - 1)
    return _toeplitz_tiles(vals, DIL_TQ, 1, width).reshape(3, DIL_HEADS, DIL_TQ, width)


def _diff_kernel(q0_ref, q1_ref, k0_ref, k1_ref, v_ref, bias_ref, lam_ref, o_ref,
                 s_scr, mx_scr, vext_scr, acc_scr, *, lam_init, nkc):
    i = pl.program_id(2)
    tq = q0_ref.shape[0]
    nrb = tq // DIFF_TK

    @pl.when(i == 0)
    def _():
        for a in range(2):
            vext_scr[a, :, 0:LANES] = v_ref[:, a * LANES:(a + 1) * LANES]
            vext_scr[a, :, LANES:2 * LANES] = jnp.ones((vext_scr.shape[1], LANES), vext_scr.dtype)

    lp = lam_ref[...]
    lam = (jnp.exp(jnp.sum(lp[0:1] * lp[1:2], axis=-1, keepdims=True))
           - jnp.exp(jnp.sum(lp[2:3] * lp[3:4], axis=-1, keepdims=True)) + lam_init)

    lane = lax.broadcasted_iota(jnp.int32, (tq, LANES), 1)
    q_refs = (q0_ref, q1_ref)
    k_refs = (k0_ref, k1_ref)

    def masked_q(u):
        a, m = divmod(u, 2)
        head_lanes = (lane >= a * DIFF_QK_DIM) & (lane < (a + 1) * DIFF_QK_DIM)
        q = q_refs[m][...]
        return jnp.where(head_lanes, q, jnp.zeros_like(q))

    def qk_chunk(u, qm, j):
        a, m = divmod(u, 2)
        s = _dot_nt(qm, k_refs[m][j * DIFF_TK:(j + 1) * DIFF_TK, :])
        for rb in range(nrb):
            rows = slice(rb * DIFF_TK, (rb + 1) * DIFF_TK)
            sb = s[rows] + bias_ref[a, j - nrb * i - rb + (nkc - 1)]
            s_scr[u, j, rows, :] = sb
            mtile = jnp.maximum(sb[:, :LANES], sb[:, LANES:])
            if j == 0:
                mx_scr[u, rows, :] = mtile
            else:
                mx_scr[u, rows, :] = jnp.maximum(mx_scr[u, rows, :], mtile)

    def row_max(u):
        return jnp.broadcast_to(jnp.max(mx_scr[u], axis=-1, keepdims=True), (tq, DIFF_TK))

    def pv_chunk(u, mb, j, acc):
        e = jnp.exp2(s_scr[u, j] - mb).astype(vext_scr.dtype)
        part = jnp.dot(e, vext_scr[u // 2, j * DIFF_TK:(j + 1) * DIFF_TK, :],
                       preferred_element_type=jnp.float32)
        return part if acc is None else acc + part

    def finish_head(a):
        n0 = acc_scr[2 * a]
        n1 = acc_scr[2 * a + 1]
        out = n0[:, :LANES] / n0[:, LANES:] - lam * (n1[:, :LANES] / n1[:, LANES:])
        ms = jnp.mean(out * out, axis=-1, keepdims=True)
        o_ref[:, a * LANES:(a + 1) * LANES] = (out * lax.rsqrt(ms + EPS) * (1.0 - lam_init)).astype(o_ref.dtype)

    n_units = 4
    mb_prev = None
    for u in range(n_units + 1):
        qm = masked_q(u) if u < n_units else None
        acc = None
        for j in range(nkc):
            if u > 0:
                acc = pv_chunk(u - 1, mb_prev, j, acc)
            if u < n_units:
                qk_chunk(u, qm, j)
        if u > 0:
            acc_scr[u - 1] = acc
            if (u - 1) % 2 == 1:
                finish_head((u - 1) // 2)
        if u < n_units:
            mb_prev = row_max(u)


def _diff_attention(zd, bias_tiles, lam_p, lam_init):
    b, s, _ = zd.shape
    tq = DIFF_TQ
    nkc = s // DIFF_TK
    kern = functools.partial(_diff_kernel, lam_init=lam_init, nkc=nkc)
    kblk = OFF_DK // LANES
    vblk = OFF_DV // (2 * LANES)
    return pl.pallas_call(
        kern,
        grid=(DIFF_HEADS // 2, b, s // tq),
        in_specs=[
            pl.BlockSpec((None, tq, LANES), lambda hp, bb, i: (bb, i, hp)),
            pl.BlockSpec((None, tq, LANES), lambda hp, bb, i: (bb, i, DIFF_HEADS // 2 + hp)),
            pl.BlockSpec((None, s, LANES), lambda hp, bb, i: (bb, 0, kblk + hp)),
            pl.BlockSpec((None, s, LANES), lambda hp, bb, i: (bb, 0, kblk + DIFF_HEADS // 2 + hp)),
            pl.BlockSpec((None, s, 2 * LANES), lambda hp, bb, i: (bb, 0, vblk + hp)),
            pl.BlockSpec((2, 2 * nkc - 1, DIFF_TK, DIFF_TK), lambda hp, bb, i: (hp, 0, 0, 0)),
            pl.BlockSpec((4, DIFF_QK_DIM), lambda hp, bb, i: (0, 0)),
        ],
        out_specs=pl.BlockSpec((None, tq, 2 * LANES), lambda hp, bb, i: (bb, i, hp)),
        out_shape=jax.ShapeDtypeStruct((b, s, DIFF_HEADS * DIFF_V_DIM), jnp.bfloat16),
        scratch_shapes=[
            pltpu.VMEM((4, nkc, tq, DIFF_TK), jnp.float32),
            pltpu.VMEM((4, tq, LANES), jnp.float32),
            pltpu.VMEM((2, s, 2 * LANES), jnp.bfloat16),
            pltpu.VMEM((4, tq, 2 * LANES), jnp.float32),
        ],
        compiler_params=_params(3),
        name="diff_attn",
    )(zd, zd, zd, zd, zd, bias_tiles, lam_p)


def _dil_kernel(z_ref, bias_ref, o_ref, lse_ref, *, r, length, width):
    nqb = length // DIL_TQ
    lane = lax.broadcasted_iota(jnp.int32, (DIL_TQ, LANES), 1)

    def block_coords(t):
        if nqb == 1:
            return t, 0
        if r == 1:
            return 0, t
        return t // nqb, t % nqb

    def body(tt, carry):
        staged = []
        for i in range(DIL_UNROLL):
            c, qb = block_coords(tt * DIL_UNROLL + i)
            q0 = pl.multiple_of(qb * DIL_TQ, DIL_TQ)
            ws = pl.multiple_of(jnp.clip(q0 - DIL_HALF, 0, length - width), DIL_HALF)
            var = jnp.where(qb == 0, 0, jnp.where(qb == nqb - 1, 2, 1))
            scs = []
            for h in range(DIL_HEADS):
                col = h * HEAD_DIM
                q = z_ref[c, pl.ds(q0, DIL_TQ), col:col + HEAD_DIM]
                kw = z_ref[c, pl.ds(ws, width), col + DIL_WIDTH:col + DIL_WIDTH + HEAD_DIM]
                scs.append(_dot_nt(q, kw) + bias_ref[var, h])
            staged.append((c, q0, ws, scs))
        for c, q0, ws, scs in staged:
            out_rows = pl.ds(q0 * r + c, DIL_TQ, stride=r) if r > 1 else pl.ds(q0, DIL_TQ)
            lse_tile = jnp.zeros((DIL_TQ, LANES), jnp.float32)
            for h in range(DIL_HEADS):
                col = 2 * DIL_WIDTH + h * HEAD_DIM
                vw = z_ref[c, pl.ds(ws, width), col:col + HEAD_DIM]
                sc = scs[h]
                mrow = jnp.max(sc, axis=-1, keepdims=True)
                e = jnp.exp2(sc - mrow)
                den = jnp.sum(e, axis=-1, keepdims=True)
                o_ref[h, out_rows, :] = jnp.dot(e.astype(vw.dtype), vw,
                                                preferred_element_type=jnp.float32) / den
                lse_tile = jnp.where(lane == h, mrow + jnp.log2(den), lse_tile)
            lse_ref[out_rows, :] = lse_tile
        return carry
    lax.fori_loop(0, r * nqb // DIL_UNROLL, body, 0)


def _dil_attention(zl, bias_tiles, g):
    b, r, length, zc = zl.shape
    s = length * r
    width = bias_tiles.shape[-1]
    kern = functools.partial(_dil_kernel, r=r, length=length, width=width)
    return pl.pallas_call(
        kern,
        grid=(b,),
        in_specs=[pl.BlockSpec((None, r, length, zc), lambda bb: (bb, 0, 0, 0)),
                  pl.BlockSpec(bias_tiles.shape, lambda bb: (0, 0, 0, 0))],
        out_specs=[pl.BlockSpec((None, DIL_HEADS, s, HEAD_DIM), lambda bb: (bb, 0, 0, 0)),
                   pl.BlockSpec((None, s, LANES), lambda bb: (bb, 0, 0))],
        out_shape=[jax.ShapeDtypeStruct((b, DIL_HEADS, s, HEAD_DIM), jnp.float32),
                   jax.ShapeDtypeStruct((b, s, LANES), jnp.float32)],
        compiler_params=_params(1),
        name=f"dil_attn_g{g}",
    )(zl, bias_tiles)


def _finish_kernel(x_ref, oa_ref, ob0_ref, ob1_ref, ob2_ref, l0_ref, l1_ref, l2_ref, mq_ref, zg_ref,
                   mkv_ref, wa_ref, wb_ref, wm_ref, wo_ref, g_ref, *out_refs, final):
    bf16 = jnp.bfloat16
    f32 = jnp.float32

    def half_silu(lo, hi):
        zh = zg_ref[:, lo:hi].astype(f32)
        return zh + zh * jnp.tanh(zh)

    ya = jnp.dot((oa_ref[...].astype(f32) * half_silu(0, 1024)).astype(bf16), wa_ref[...],
                 preferred_element_type=jnp.float32)

    l0, l1, l2 = l0_ref[...], l1_ref[...], l2_ref[...]
    lmax = jnp.maximum(jnp.maximum(l0, l1), l2)
    w0, w1, w2 = jnp.exp2(l0 - lmax), jnp.exp2(l1 - lmax), jnp.exp2(l2 - lmax)
    inv = 1.0 / (w0 + w1 + w2)
    w0, w1, w2 = w0 * inv, w1 * inv, w2 * inv
    ob_refs = (ob0_ref, ob1_ref, ob2_ref)
    parts = []
    for h in range(DIL_HEADS):
        acc = None
        for wg, ob_ref in zip((w0, w1, w2), ob_refs):
            term = wg[:, h:h + 1] * ob_ref[h]
            acc = term if acc is None else acc + term
        parts.append(acc)
    ob = jnp.concatenate(parts, axis=-1)
    yb = jnp.dot((ob * half_silu(1024, 1536)).astype(bf16), wb_ref[...], preferred_element_type=jnp.float32)

    parts = []
    for h in range(MEM_HEADS):
        cols = slice(h * HEAD_DIM, (h + 1) * HEAD_DIM)
        sc = _dot_nt(mq_ref[:, cols], mkv_ref[:, cols])
        e = jnp.exp2(sc - jnp.max(sc, axis=-1, keepdims=True))
        den = jnp.sum(e, axis=-1, keepdims=True)
        vcols = slice(MEM_WIDTH + h * HEAD_DIM, MEM_WIDTH + (h + 1) * HEAD_DIM)
        parts.append(jnp.dot(e.astype(bf16), mkv_ref[:, vcols], preferred_element_type=jnp.float32) / den)
    om = jnp.concatenate(parts, axis=-1)
    ym = jnp.dot((om * half_silu(1536, 2048)).astype(bf16), wm_ref[...], preferred_element_type=jnp.float32)

    merged = (ya + yb + ym
              + jnp.tanh(zg_ref[:, 2048:3072].astype(f32)) * ya
              + jnp.tanh(zg_ref[:, 3072:4096].astype(f32)) * yb
              + jnp.tanh(zg_ref[:, 4096:5120].astype(f32)) * ym)
    xn = x_ref[...] + jnp.dot(merged.astype(bf16), wo_ref[...], preferred_element_type=jnp.float32)
    hn = xn * lax.rsqrt(jnp.mean(xn * xn, axis=-1, keepdims=True) + EPS) * g_ref[...]
    if final:
        out_refs[0][...] = hn
    else:
        out_refs[0][...] = xn
        out_refs[1][...] = hn.astype(out_refs[1].dtype)


def _finish(x, oa, obs, lses, zd, zg, mkv, wa, wb, wm, wo, g_next, final):
    b, s, d = x.shape
    t = FIN_T
    row = lambda width: pl.BlockSpec((None, t, width), lambda bb, i: (bb, i, 0))
    full = lambda arr: pl.BlockSpec(arr.shape, lambda bb, i: (0,) * arr.ndim)
    heads = pl.BlockSpec((None, DIL_HEADS, t, HEAD_DIM), lambda bb, i: (bb, 0, i, 0))
    in_specs = [row(d), row(d), heads, heads, heads,
                row(LANES), row(LANES), row(LANES),
                pl.BlockSpec((None, t, MEM_WIDTH), lambda bb, i: (bb, i, 3072 // MEM_WIDTH)),
                row(zg.shape[-1]),
                pl.BlockSpec((None, N_MEM, 2 * MEM_WIDTH), lambda bb, i: (bb, 0, 0)),
                full(wa), full(wb), full(wm), full(wo),
                pl.BlockSpec((1, d), lambda bb, i: (0, 0))]
    if final:
        out_specs = [row(d)]
        out_shape = [jax.ShapeDtypeStruct((b, s, d), jnp.float32)]
    else:
        out_specs = [row(d), row(d)]
        out_shape = [jax.ShapeDtypeStruct((b, s, d), jnp.float32),
                     jax.ShapeDtypeStruct((b, s, d), jnp.bfloat16)]
    return pl.pallas_call(
        functools.partial(_finish_kernel, final=final),
        grid=(b, s // t),
        in_specs=in_specs,
        out_specs=out_specs,
        out_shape=out_shape,
        compiler_params=_params(2),
        name="finish",
    )(x, oa, *obs, *lses, zd, zg, mkv, wa, wb, wm, wo, g_next.reshape(1, d))


def kernel(x, mem, g_norm, w_in, diff_lambda, w_mem_kv, g_mem, w_br_diff, w_br_dil, w_br_mem, w_out,
           rel_bias, g_final):
    b, s, d = x.shape
    depth = w_in.shape[0]
    bf16 = jnp.bfloat16
    m_rows = b * s

    diff_tiles = _diff_bias_tiles(rel_bias, s)
    dil_tiles = [_dil_bias_tiles(rel_bias, g, s) for g in range(len(DIL_GROUPS))]
    mem2d = mem.reshape(b * N_MEM, d)

    c_diff = DIFF_QK_DIM ** -0.5 * LOG2E
    c_head = HEAD_DIM ** -0.5 * LOG2E

    h = _rmsnorm(x.reshape(m_rows, d), g_norm[0], bf16)
    out = None
    for l in range(depth):
        w = w_in[l]
        w_d = jnp.concatenate([w[:, OFF_DQ:OFF_DK] * c_diff, w[:, OFF_DK:OFF_DG],
                               w[:, OFF_MQ:OFF_MG] * c_head], axis=1).astype(bf16)
        w_l = [jnp.concatenate([w[:, OFF_LQ + g * DIL_WIDTH:OFF_LQ + (g + 1) * DIL_WIDTH] * c_head,
                                w[:, OFF_LK + g * DIL_WIDTH:OFF_LK + (g + 1) * DIL_WIDTH],
                                w[:, OFF_LV + g * DIL_WIDTH:OFF_LV + (g + 1) * DIL_WIDTH]],
                               axis=1).astype(bf16) for g in range(len(DIL_GROUPS))]
        w_g = (0.5 * jnp.concatenate([w[:, OFF_DG:OFF_LQ], w[:, OFF_LG:OFF_MQ], w[:, OFF_MG:]],
                                     axis=1)).astype(bf16)

        zd = _matmul(h, w_d, bf16, tn=1792).reshape(b, s, -1)
        zls = [_matmul_strided(h, w_l[g], b, DIL_GROUPS[g][1], bf16) for g in range(len(DIL_GROUPS))]
        zg = _matmul(h, w_g, bf16, tn=1024).reshape(b, s, -1)

        mem_n = _rmsnorm(mem2d, g_mem[l], bf16)
        mkv = _matmul(mem_n, w_mem_kv[l].astype(bf16), bf16, tn=1024).reshape(b, N_MEM, 2 * MEM_WIDTH)

        lam_init = 0.8 - 0.6 * math.exp(-0.3 * l)
        oa = _diff_attention(zd, diff_tiles, diff_lambda[l], lam_init)
        obs, lses = zip(*[_dil_attention(zls[g], dil_tiles[g], g) for g in range(len(DIL_GROUPS))])

        final = l == depth - 1
        g_next = g_final if final else g_norm[l + 1]
        res = _finish(x, oa, obs, lses, zd, zg, mkv, w_br_diff[l].astype(bf16), w_br_dil[l].astype(bf16),
                      w_br_mem[l].astype(bf16), (0.5 * w_out[l]).astype(bf16), g_next, final)
        if final:
            out = res[0]
        else:
            x, h3 = res
            h = h3.reshape(m_rows, d)
    return out
```

```python
import functools
import math

import jax
import jax.numpy as jnp
import numpy as np
from jax import lax
from jax.experimental import pallas as pl
from jax.experimental.pallas import tpu as pltpu

D_MODEL = 1024
N_MEM = 256
EPS = 1e-6
NEG_INF = -1e30

DIFF_HEADS = 8
DIFF_QK_DIM = 64
DIFF_V_DIM = 128
DIL_GROUPS = ((128, 1), (512, 4), (2048, 16))
DIL_HEADS = 4
HEAD_DIM = 128
DIL_WIDTH = DIL_HEADS * HEAD_DIM
DIL_HALF = 64
MEM_HEADS = 4
MEM_WIDTH = MEM_HEADS * HEAD_DIM
REL_BUCKETS = 32
REL_MAX_DIST = 1024

OFF_DQ, OFF_DK, OFF_DV, OFF_DG = 0, 1024, 2048, 3072
OFF_LQ, OFF_LK, OFF_LV, OFF_LG = 4096, 5632, 7168, 8704
OFF_MQ, OFF_MG, OFF_MGATE = 9216, 9728, 10240
N_IN = 13312

LOG2E = 1.4426950408889634
LN2 = 0.6931471805599453

LANES = 128
MXU_EDGE = 256
VMEM_LIMIT_BYTES = 56 * 1024 * 1024

DIFF_TQ = 512
DIFF_TK = MXU_EDGE
DIL_TQ = 128
DIL_UNROLL = 4
FIN_T = 256
MM_TM = 2048
MM_SUB = 512
W_BLOCK = 512
STRIDE_STEP = 4


def _params(n_grid_dims):
    return pltpu.CompilerParams(dimension_semantics=("arbitrary",) * n_grid_dims,
                                vmem_limit_bytes=VMEM_LIMIT_BYTES)


def _dot_nt(a, b):
    return lax.dot_general(a, b, (((1,), (1,)), ((), ())), preferred_element_type=jnp.float32)


def _rms_kernel(x_ref, g_ref, o_ref):
    x = x_ref[...]
    ms = jnp.mean(x * x, axis=-1, keepdims=True)
    o_ref[...] = (x * lax.rsqrt(ms + EPS) * g_ref[...]).astype(o_ref.dtype)


def _rmsnorm(x2d, g, out_dtype, tm=512):
    m, d = x2d.shape
    tm = min(tm, m)
    return pl.pallas_call(
        _rms_kernel,
        grid=(m // tm,),
        in_specs=[pl.BlockSpec((tm, d), lambda i: (i, 0)),
                  pl.BlockSpec((1, d), lambda i: (0, 0))],
        out_specs=pl.BlockSpec((tm, d), lambda i: (i, 0)),
        out_shape=jax.ShapeDtypeStruct((m, d), out_dtype),
        compiler_params=_params(1),
        name="rmsnorm",
    )(x2d, g.reshape(1, d))


def _mm_kernel(a_ref, w_ref, o_ref):
    sub = min(MM_SUB, a_ref.shape[0])

    def body(s, carry):
        r0 = pl.multiple_of(s * sub, sub)
        acc = jnp.dot(a_ref[pl.ds(r0, sub), :], w_ref[...], preferred_element_type=jnp.float32)
        o_ref[pl.ds(r0, sub), :] = acc.astype(o_ref.dtype)
        return carry
    lax.fori_loop(0, a_ref.shape[0] // sub, body, 0)


def _matmul(a, w, out_dtype, tn):
    m, k = a.shape
    n = w.shape[1]
    tm = min(MM_TM, m)
    return pl.pallas_call(
        _mm_kernel,
        grid=(m // tm, n // tn),
        in_specs=[pl.BlockSpec((tm, k), lambda i, j: (i, 0)),
                  pl.BlockSpec((k, tn), lambda i, j: (0, j))],
        out_specs=pl.BlockSpec((tm, tn), lambda i, j: (i, j)),
        out_shape=jax.ShapeDtypeStruct((m, n), out_dtype),
        compiler_params=_params(2),
        name="in_proj",
    )(a, w)


def _cast_weight_blocks(w_refs, s_refs, wbf_scr):
    for t, (w_ref, s_ref) in enumerate(zip(w_refs, s_refs)):
        wbf_scr[:, t * W_BLOCK:(t + 1) * W_BLOCK] = (w_ref[...] * s_ref[...]).astype(wbf_scr.dtype)


def _proj_kernel(a_ref, *refs, nb):
    w_refs, s_refs, o_ref, wbf_scr = refs[:nb], refs[nb:2 * nb], refs[2 * nb], refs[2 * nb + 1]

    @pl.when(pl.program_id(1) == 0)
    def _():
        _cast_weight_blocks(w_refs, s_refs, wbf_scr)

    sub = min(MM_SUB, a_ref.shape[0])

    def body(s, carry):
        r0 = pl.multiple_of(s * sub, sub)
        acc = jnp.dot(a_ref[pl.ds(r0, sub), :], wbf_scr[...], preferred_element_type=jnp.float32)
        o_ref[pl.ds(r0, sub), :] = acc.astype(o_ref.dtype)
        return carry
    lax.fori_loop(0, a_ref.shape[0] // sub, body, 0, unroll=True)


def _weight_specs(layer, blocks, n_grid):
    def spec(t, weight):
        if n_grid == 2:
            imap = (lambda j, i: (layer, 0, blocks(j, t))) if weight else (lambda j, i: (0, blocks(j, t)))
        else:
            imap = (lambda i: (layer, 0, blocks(0, t))) if weight else (lambda i: (0, blocks(0, t)))
        return pl.BlockSpec((None, D_MODEL, W_BLOCK) if weight else (1, W_BLOCK), imap)
    return spec


def _project(a, w_in, scales, layer, blocks, nb, n_tiles, out_dtype):
    m, k = a.shape
    tm = min(MM_TM, m)
    tn = nb * W_BLOCK
    spec = _weight_specs(layer, blocks, 2)
    return pl.pallas_call(
        functools.partial(_proj_kernel, nb=nb),
        grid=(n_tiles, m // tm),
        in_specs=([pl.BlockSpec((tm, k), lambda j, i: (i, 0))]
                  + [spec(t, True) for t in range(nb)] + [spec(t, False) for t in range(nb)]),
        out_specs=pl.BlockSpec((tm, tn), lambda j, i: (i, j)),
        out_shape=jax.ShapeDtypeStruct((m, n_tiles * tn), out_dtype),
        scratch_shapes=[pltpu.VMEM((k, tn), jnp.bfloat16)],
        compiler_params=_params(2),
        name="in_proj",
    )(a, *([w_in] * nb), *([scales] * nb))


def _proj_strided_kernel(a_ref, *refs, nb, r):
    w_refs, s_refs = refs[:nb], refs[nb:2 * nb]
    o_ref, wbf_scr, scr, tmp = refs[2 * nb:]

    @pl.when(pl.program_id(0) == 0)
    def _():
        _cast_weight_blocks(w_refs, s_refs, wbf_scr)

    rows = a_ref.shape[0]
    length = rows // r
    n_slabs = wbf_scr.shape[1] // LANES
    sub = min(MM_SUB, rows)

    def body(s, carry):
        r0 = pl.multiple_of(s * sub, sub)
        acc = jnp.dot(a_ref[pl.ds(r0, sub), :], wbf_scr[...], preferred_element_type=jnp.float32)
        if r == 1:
            o_ref[0, pl.ds(r0, sub), :] = acc.astype(o_ref.dtype)
        else:
            for k in range(n_slabs):
                scr[k, pl.ds(r0, sub), :] = acc[:, k * LANES:(k + 1) * LANES]
        return carry
    lax.fori_loop(0, rows // sub, body, 0, unroll=(r == 1))

    if r > STRIDE_STEP:
        r2 = r // STRIDE_STEP
        part = rows // STRIDE_STEP
        for k in range(n_slabs):
            for c1 in range(STRIDE_STEP):
                tmp[c1] = scr[k, pl.ds(c1, part, stride=STRIDE_STEP), :]
            for c1 in range(STRIDE_STEP):
                for c2 in range(r2):
                    o_ref[c1 + STRIDE_STEP * c2, :, k * LANES:(k + 1) * LANES] = (
                        tmp[c1, pl.ds(c2, length, stride=r2), :].astype(o_ref.dtype))
    elif r > 1:
        for c in range(r):
            for k in range(n_slabs):
                o_ref[c, :, k * LANES:(k + 1) * LANES] = scr[k, pl.ds(c, length, stride=r), :].astype(o_ref.dtype)


def _project_strided(a, w_in, scales, layer, blocks, nb, batch, r, out_dtype):
    m, k = a.shape
    n = nb * W_BLOCK
    seq = m // batch
    spec = _weight_specs(layer, blocks, 1)
    return pl.pallas_call(
        functools.partial(_proj_strided_kernel, nb=nb, r=r),
        grid=(batch,),
        in_specs=([pl.BlockSpec((seq, k), lambda i: (i, 0))]
                  + [spec(t, True) for t in range(nb)] + [spec(t, False) for t in range(nb)]),
        out_specs=pl.BlockSpec((None, r, seq // r, n), lambda i: (i, 0, 0, 0)),
        out_shape=jax.ShapeDtypeStruct((batch, r, seq // r, n), out_dtype),
        scratch_shapes=[pltpu.VMEM((k, n), jnp.bfloat16),
                        pltpu.VMEM((n // LANES, seq if r > 1 else 8, LANES), jnp.float32),
                        pltpu.VMEM((STRIDE_STEP, seq // STRIDE_STEP if r > STRIDE_STEP else 8, LANES),
                                   jnp.float32)],
        compiler_params=_params(1),
        name=f"in_proj_stride{r}",
    )(a, *([w_in] * nb), *([scales] * nb))


def _t5_bucket(rel):
    half = REL_BUCKETS // 2
    max_exact = half // 2
    ret = jnp.where(rel > 0, half, 0)
    n = jnp.abs(rel)
    nf = jnp.maximum(n, 1).astype(jnp.float32)
    large = max_exact + (jnp.log(nf / max_exact) / math.log(REL_MAX_DIST / max_exact)
                         * (half - max_exact)).astype(jnp.int32)
    large = jnp.minimum(large, half - 1)
    return ret + jnp.where(n < max_exact, n, large)


def _toeplitz_kernel(u_ref, o_ref):
    n_tiles, rows, tile_w = o_ref.shape
    x = jnp.broadcast_to(u_ref[...], (rows, u_ref.shape[-1]))
    y = pltpu.roll(x, 0, 1, stride=1, stride_axis=0)
    for d in range(n_tiles):
        o_ref[d] = y[:, d * tile_w:(d + 1) * tile_w]


def _toeplitz_tiles(vals, rows, n_tiles, tile_w):
    groups = vals.shape[0]
    cols = n_tiles * tile_w
    period = pl.next_power_of_2(rows + cols - 1)
    pad = jnp.zeros((groups, period - (rows + cols - 1)), vals.dtype)
    u = jnp.concatenate([vals[:, rows - 1:], pad, vals[:, :rows - 1]], axis=1).reshape(groups, 1, period)
    return pl.pallas_call(
        _toeplitz_kernel,
        grid=(groups,),
        in_specs=[pl.BlockSpec((None, 1, period), lambda g: (g, 0, 0))],
        out_specs=pl.BlockSpec((None, n_tiles, rows, tile_w), lambda g: (g, 0, 0, 0)),
        out_shape=jax.ShapeDtypeStruct((groups, n_tiles, rows, tile_w), vals.dtype),
        compiler_params=_params(1),
        name="toeplitz_tiles",
    )(u)


def _diff_bias_tiles(rel_bias, seq):
    nd = seq // DIFF_TK - 1
    rel = jnp.arange(2 * seq - 1, dtype=jnp.int32) - (seq - 1)
    tvec = jnp.take(rel_bias[:, :DIFF_HEADS], _t5_bucket(rel), axis=0).T * LOG2E
    return _toeplitz_tiles(tvec, DIFF_TK, 2 * nd + 1, DIFF_TK)


def _dil_bias_tiles(rel_bias, g, seq):
    _, r = DIL_GROUPS[g]
    length = seq // r
    width = min(2 * DIL_TQ, length)
    shifts = jnp.array([0, -DIL_HALF, -(width - DIL_TQ)], dtype=jnp.int32)
    delta = shifts[:, None] + jnp.arange(DIL_TQ + width - 1, dtype=jnp.int32)[None, :] - (DIL_TQ - 1)
    c0 = DIFF_HEADS + g * DIL_HEADS
    bias = jnp.take(rel_bias[:, c0:c0 + DIL_HEADS], _t5_bucket(delta * r), axis=0) * LOG2E
    bias = jnp.where((jnp.abs(delta) <= DIL_HALF)[..., None], bias, NEG_INF)
    vals = jnp.transpose(bias, (0, 2, 1)).reshape(3 * DIL_HEADS, DIL_TQ + width - 1)
    return _toeplitz_tiles(vals, DIL_TQ, 1, width).reshape(3, DIL_HEADS, DIL_TQ, width)


def _diff_kernel(q0_ref, q1_ref, k0_ref, k1_ref, v_ref, bias_ref, lam_ref, o_ref,
                 s_scr, mx_scr, vext_scr, acc_scr, *, lam_init, nkc):
    i = pl.program_id(2)
    tq = q0_ref.shape[0]
    nrb = tq // DIFF_TK

    @pl.when(i == 0)
    def _():
        for a in range(2):
            vext_scr[a, :, 0:LANES] = v_ref[:, a * LANES:(a + 1) * LANES]
            vext_scr[a, :, LANES:2 * LANES] = jnp.ones((vext_scr.shape[1], LANES), vext_scr.dtype)

    lp = lam_ref[...]
    lam = (jnp.exp(jnp.sum(lp[0:1] * lp[1:2], axis=-1, keepdims=True))
           - jnp.exp(jnp.sum(lp[2:3] * lp[3:4], axis=-1, keepdims=True)) + lam_init)

    lane = lax.broadcasted_iota(jnp.int32, (tq, LANES), 1)
    q_refs = (q0_ref, q1_ref)
    k_refs = (k0_ref, k1_ref)

    def masked_q(u):
        a, m = divmod(u, 2)
        head_lanes = (lane >= a * DIFF_QK_DIM) & (lane < (a + 1) * DIFF_QK_DIM)
        q = q_refs[m][...]
        return jnp.where(head_lanes, q, jnp.zeros_like(q))

    def qk_chunk(u, qm, j):
        a, m = divmod(u, 2)
        s = _dot_nt(qm, k_refs[m][j * DIFF_TK:(j + 1) * DIFF_TK, :])
        for rb in range(nrb):
            rows = slice(rb * DIFF_TK, (rb + 1) * DIFF_TK)
            sb = s[rows] + bias_ref[a, j - nrb * i - rb + (nkc - 1)]
            s_scr[u, j, rows, :] = sb
            mtile = jnp.maximum(sb[:, :LANES], sb[:, LANES:])
            if j == 0:
                mx_scr[u, rows, :] = mtile
            else:
                mx_scr[u, rows, :] = jnp.maximum(mx_scr[u, rows, :], mtile)

    def row_max(u):
        return jnp.broadcast_to(jnp.max(mx_scr[u], axis=-1, keepdims=True), (tq, DIFF_TK))

    def pv_chunk(u, mb, j, acc):
        e = jnp.exp2(s_scr[u, j] - mb).astype(vext_scr.dtype)
        part = jnp.dot(e, vext_scr[u // 2, j * DIFF_TK:(j + 1) * DIFF_TK, :],
                       preferred_element_type=jnp.float32)
        return part if acc is None else acc + part

    def finish_head(a):
        n0 = acc_scr[2 * a]
        n1 = acc_scr[2 * a + 1]
        out = n0[:, :LANES] / n0[:, LANES:] - lam * (n1[:, :LANES] / n1[:, LANES:])
        ms = jnp.mean(out * out, axis=-1, keepdims=True)
        o_ref[:, a * LANES:(a + 1) * LANES] = (out * lax.rsqrt(ms + EPS) * (1.0 - lam_init)).astype(o_ref.dtype)

    n_units = 4
    mb_prev = None
    for u in range(n_units + 1):
        qm = masked_q(u) if u < n_units else None
        acc = None
        for j in range(nkc):
            if u < n_units:
                qk_chunk(u, qm, j)
            if u > 0:
                acc = pv_chunk(u - 1, mb_prev, j, acc)
        if u > 0:
            acc_scr[u - 1] = acc
            if (u - 1) % 2 == 1:
                finish_head((u - 1) // 2)
        if u < n_units:
            mb_prev = row_max(u)


def _diff_attention(zd, bias_tiles, lam_p, lam_init):
    b, s, _ = zd.shape
    tq = DIFF_TQ
    nkc = s // DIFF_TK
    kern = functools.partial(_diff_kernel, lam_init=lam_init, nkc=nkc)
    kblk = OFF_DK // LANES
    vblk = OFF_DV // (2 * LANES)
    return pl.pallas_call(
        kern,
        grid=(DIFF_HEADS // 2, b, s // tq),
        in_specs=[
            pl.BlockSpec((None, tq, LANES), lambda hp, bb, i: (bb, i, hp)),
            pl.BlockSpec((None, tq, LANES), lambda hp, bb, i: (bb, i, DIFF_HEADS // 2 + hp)),
            pl.BlockSpec((None, s, LANES), lambda hp, bb, i: (bb, 0, kblk + hp)),
            pl.BlockSpec((None, s, LANES), lambda hp, bb, i: (bb, 0, kblk + DIFF_HEADS // 2 + hp)),
            pl.BlockSpec((None, s, 2 * LANES), lambda hp, bb, i: (bb, 0, vblk + hp)),
            pl.BlockSpec((2, 2 * nkc - 1, DIFF_TK, DIFF_TK), lambda hp, bb, i: (hp, 0, 0, 0)),
            pl.BlockSpec((4, DIFF_QK_DIM), lambda hp, bb, i: (0, 0)),
        ],
        out_specs=pl.BlockSpec((None, tq, 2 * LANES), lambda hp, bb, i: (bb, i, hp)),
        out_shape=jax.ShapeDtypeStruct((b, s, DIFF_HEADS * DIFF_V_DIM), jnp.bfloat16),
        scratch_shapes=[
            pltpu.VMEM((4, nkc, tq, DIFF_TK), jnp.float32),
            pltpu.VMEM((4, tq, LANES), jnp.float32),
            pltpu.VMEM((2, s, 2 * LANES), jnp.bfloat16),
            pltpu.VMEM((4, tq, 2 * LANES), jnp.float32),
        ],
        compiler_params=_params(3),
        name="diff_attn",
    )(zd, zd, zd, zd, zd, bias_tiles, lam_p)


def _dil_kernel(z_ref, bias_ref, o_ref, lse_ref, *, r, length, width):
    nqb = length // DIL_TQ
    lane = lax.broadcasted_iota(jnp.int32, (DIL_TQ, LANES), 1)

    def block_coords(t):
        if nqb == 1:
            return t, 0
        if r == 1:
            return 0, t
        return t // nqb, t % nqb

    def body(tt, carry):
        staged = []
        for i in range(DIL_UNROLL):
            c, qb = block_coords(tt * DIL_UNROLL + i)
            q0 = pl.multiple_of(qb * DIL_TQ, DIL_TQ)
            ws = pl.multiple_of(jnp.clip(q0 - DIL_HALF, 0, length - width), DIL_HALF)
            var = jnp.where(qb == 0, 0, jnp.where(qb == nqb - 1, 2, 1))
            scs = []
            for h in range(DIL_HEADS):
                col = h * HEAD_DIM
                q = z_ref[c, pl.ds(q0, DIL_TQ), col:col + HEAD_DIM]
                kw = z_ref[c, pl.ds(ws, width), col + DIL_WIDTH:col + DIL_WIDTH + HEAD_DIM]
                scs.append(_dot_nt(q, kw) + bias_ref[var, h])
            staged.append((c, q0, ws, scs))
        for c, q0, ws, scs in staged:
            out_rows = pl.ds(q0 * r + c, DIL_TQ, stride=r) if r > 1 else pl.ds(q0, DIL_TQ)
            lse_tile = jnp.zeros((DIL_TQ, LANES), jnp.float32)
            for h in range(DIL_HEADS):
                col = 2 * DIL_WIDTH + h * HEAD_DIM
                vw = z_ref[c, pl.ds(ws, width), col:col + HEAD_DIM]
                sc = scs[h]
                mrow = jnp.max(sc, axis=-1, keepdims=True)
                e = jnp.exp2(sc - mrow)
                den = jnp.sum(e, axis=-1, keepdims=True)
                o_ref[h, out_rows, :] = jnp.dot(e.astype(vw.dtype), vw,
                                                preferred_element_type=jnp.float32) / den
                lse_tile = jnp.where(lane == h, mrow + jnp.log2(den), lse_tile)
            lse_ref[out_rows, :] = lse_tile
        return carry
    lax.fori_loop(0, r * nqb // DIL_UNROLL, body, 0)


def _dil_attention(zl, bias_tiles, g):
    b, r, length, zc = zl.shape
    s = length * r
    width = bias_tiles.shape[-1]
    kern = functools.partial(_dil_kernel, r=r, length=length, width=width)
    return pl.pallas_call(
        kern,
        grid=(b,),
        in_specs=[pl.BlockSpec((None, r, length, zc), lambda bb: (bb, 0, 0, 0)),
                  pl.BlockSpec(bias_tiles.shape, lambda bb: (0, 0, 0, 0))],
        out_specs=[pl.BlockSpec((None, DIL_HEADS, s, HEAD_DIM), lambda bb: (bb, 0, 0, 0)),
                   pl.BlockSpec((None, s, LANES), lambda bb: (bb, 0, 0))],
        out_shape=[jax.ShapeDtypeStruct((b, DIL_HEADS, s, HEAD_DIM), jnp.float32),
                   jax.ShapeDtypeStruct((b, s, LANES), jnp.float32)],
        compiler_params=_params(1),
        name=f"dil_attn_g{g}",
    )(zl, bias_tiles)


def _finish_kernel(x_ref, oa_ref, ob0_ref, ob1_ref, ob2_ref, l0_ref, l1_ref, l2_ref, dg_ref, zg_ref,
                   mkv_ref, wa_ref, wb_ref, wm_ref, wo_ref, g_ref, *out_refs, final):
    bf16 = jnp.bfloat16
    f32 = jnp.float32

    def half_silu(zh):
        zh = zh.astype(f32)
        return zh + zh * jnp.tanh(zh)

    c_lg, c_mq, c_mg, c_gate = 0, DIL_WIDTH, DIL_WIDTH + MEM_WIDTH, DIL_WIDTH + 2 * MEM_WIDTH

    ya = jnp.dot((oa_ref[...].astype(f32) * half_silu(dg_ref[...])).astype(bf16), wa_ref[...],
                 preferred_element_type=jnp.float32)

    l0, l1, l2 = l0_ref[...], l1_ref[...], l2_ref[...]
    lmax = jnp.maximum(jnp.maximum(l0, l1), l2)
    w0, w1, w2 = jnp.exp2(l0 - lmax), jnp.exp2(l1 - lmax), jnp.exp2(l2 - lmax)
    inv = 1.0 / (w0 + w1 + w2)
    w0, w1, w2 = w0 * inv, w1 * inv, w2 * inv
    ob_refs = (ob0_ref, ob1_ref, ob2_ref)
    parts = []
    for h in range(DIL_HEADS):
        acc = None
        for wg, ob_ref in zip((w0, w1, w2), ob_refs):
            term = wg[:, h:h + 1] * ob_ref[h]
            acc = term if acc is None else acc + term
        parts.append(acc)
    ob = jnp.concatenate(parts, axis=-1)
    yb = jnp.dot((ob * half_silu(zg_ref[:, c_lg:c_lg + DIL_WIDTH])).astype(bf16), wb_ref[...],
                 preferred_element_type=jnp.float32)

    parts = []
    for h in range(MEM_HEADS):
        cols = slice(h * HEAD_DIM, (h + 1) * HEAD_DIM)
        sc = _dot_nt(zg_ref[:, c_mq + h * HEAD_DIM:c_mq + (h + 1) * HEAD_DIM], mkv_ref[:, cols])
        e = jnp.exp2(sc - jnp.max(sc, axis=-1, keepdims=True))
        den = jnp.sum(e, axis=-1, keepdims=True)
        vcols = slice(MEM_WIDTH + h * HEAD_DIM, MEM_WIDTH + (h + 1) * HEAD_DIM)
        parts.append(jnp.dot(e.astype(bf16), mkv_ref[:, vcols], preferred_element_type=jnp.float32) / den)
    om = jnp.concatenate(parts, axis=-1)
    ym = jnp.dot((om * half_silu(zg_ref[:, c_mg:c_mg + MEM_WIDTH])).astype(bf16), wm_ref[...],
                 preferred_element_type=jnp.float32)

    merged = (ya + yb + ym
              + jnp.tanh(zg_ref[:, c_gate:c_gate + D_MODEL].astype(f32)) * ya
              + jnp.tanh(zg_ref[:, c_gate + D_MODEL:c_gate + 2 * D_MODEL].astype(f32)) * yb
              + jnp.tanh(zg_ref[:, c_gate + 2 * D_MODEL:c_gate + 3 * D_MODEL].astype(f32)) * ym)
    xn = x_ref[...] + jnp.dot(merged.astype(bf16), wo_ref[...], preferred_element_type=jnp.float32)
    hn = xn * lax.rsqrt(jnp.mean(xn * xn, axis=-1, keepdims=True) + EPS) * g_ref[...]
    if final:
        out_refs[0][...] = hn
    else:
        out_refs[0][...] = xn
        out_refs[1][...] = hn.astype(out_refs[1].dtype)


def _finish(x, oa, obs, lses, zd, zg, mkv, wa, wb, wm, wo, g_next, final):
    b, s, d = x.shape
    t = FIN_T
    row = lambda width: pl.BlockSpec((None, t, width), lambda bb, i: (bb, i, 0))
    full = lambda arr: pl.BlockSpec(arr.shape, lambda bb, i: (0,) * arr.ndim)
    heads = pl.BlockSpec((None, DIL_HEADS, t, HEAD_DIM), lambda bb, i: (bb, 0, i, 0))
    in_specs = [row(d), row(d), heads, heads, heads,
                row(LANES), row(LANES), row(LANES),
                pl.BlockSpec((None, t, d), lambda bb, i: (bb, i, OFF_DG // D_MODEL)),
                row(zg.shape[-1]),
                pl.BlockSpec((None, N_MEM, 2 * MEM_WIDTH), lambda bb, i: (bb, 0, 0)),
                full(wa), full(wb), full(wm), full(wo),
                pl.BlockSpec((1, d), lambda bb, i: (0, 0))]
    if final:
        out_specs = [row(d)]
        out_shape = [jax.ShapeDtypeStruct((b, s, d), jnp.float32)]
    else:
        out_specs = [row(d), row(d)]
        out_shape = [jax.ShapeDtypeStruct((b, s, d), jnp.float32),
                     jax.ShapeDtypeStruct((b, s, d), jnp.bfloat16)]
    return pl.pallas_call(
        functools.partial(_finish_kernel, final=final),
        grid=(b, s // t),
        in_specs=in_specs,
        out_specs=out_specs,
        out_shape=out_shape,
        compiler_params=_params(2),
        name="finish",
    )(x, oa, *obs, *lses, zd, zg, mkv, wa, wb, wm, wo, g_next.reshape(1, d))


def kernel(x, mem, g_norm, w_in, diff_lambda, w_mem_kv, g_mem, w_br_diff, w_br_dil, w_br_mem, w_out,
           rel_bias, g_final):
    b, s, d = x.shape
    depth = w_in.shape[0]
    bf16 = jnp.bfloat16
    m_rows = b * s

    diff_tiles = _diff_bias_tiles(rel_bias, s)
    dil_tiles = [_dil_bias_tiles(rel_bias, g, s) for g in range(len(DIL_GROUPS))]
    mem2d = mem.reshape(b * N_MEM, d)

    col_scale = np.ones((1, N_IN), np.float32)
    col_scale[0, OFF_DQ:OFF_DK] = DIFF_QK_DIM ** -0.5 * LOG2E
    col_scale[0, OFF_LQ:OFF_LK] = HEAD_DIM ** -0.5 * LOG2E
    col_scale[0, OFF_MQ:OFF_MG] = HEAD_DIM ** -0.5 * LOG2E
    col_scale[0, OFF_DG:OFF_LQ] = 0.5
    col_scale[0, OFF_LG:OFF_MQ] = 0.5
    col_scale[0, OFF_MG:] = 0.5
    col_scale = jnp.asarray(col_scale)

    h = _rmsnorm(x.reshape(m_rows, d), g_norm[0], bf16)
    out = None
    for l in range(depth):
        zd = _project(h, w_in, col_scale, l, lambda j, t: 2 * j + t, 2, OFF_LQ // (2 * W_BLOCK),
                      bf16).reshape(b, s, -1)
        zg = _project(h, w_in, col_scale, l, lambda j, t: OFF_LG // W_BLOCK + 3 * j + t, 3,
                      (N_IN - OFF_LG) // (3 * W_BLOCK), bf16).reshape(b, s, -1)
        zls = [_project_strided(h, w_in, col_scale, l,
                                lambda j, t, g=g: (OFF_LQ, OFF_LK, OFF_LV)[t] // W_BLOCK + g, 3,
                                b, DIL_GROUPS[g][1], bf16) for g in range(len(DIL_GROUPS))]

        mem_n = _rmsnorm(mem2d, g_mem[l], bf16)
        mkv = _matmul(mem_n, w_mem_kv[l].astype(bf16), bf16, tn=1024).reshape(b, N_MEM, 2 * MEM_WIDTH)

        lam_init = 0.8 - 0.6 * math.exp(-0.3 * l)
        oa = _diff_attention(zd, diff_tiles, diff_lambda[l], lam_init)
        obs, lses = zip(*[_dil_attention(zls[g], dil_tiles[g], g) for g in range(len(DIL_GROUPS))])

        final = l == depth - 1
        g_next = g_final if final else g_norm[l + 1]
        res = _finish(x, oa, obs, lses, zd, zg, mkv, w_br_diff[l].astype(bf16), w_br_dil[l].astype(bf16),
                      w_br_mem[l].astype(bf16), (0.5 * w_out[l]).astype(bf16), g_next, final)
        if final:
            out = res[0]
        else:
            x, h3 = res
            h = h3.reshape(m_rows, d)
    return out
```

```python
import functools
import math

import jax
import jax.numpy as jnp
import numpy as np
from jax import lax
from jax.experimental import pallas as pl
from jax.experimental.pallas import tpu as pltpu

D_MODEL = 1024
N_MEM = 256
EPS = 1e-6
NEG_INF = -1e30

DIFF_HEADS = 8
DIFF_QK_DIM = 64
DIFF_V_DIM = 128
DIL_GROUPS = ((128, 1), (512, 4), (2048, 16))
DIL_HEADS = 4
HEAD_DIM = 128
DIL_WIDTH = DIL_HEADS * HEAD_DIM
DIL_HALF = 64
MEM_HEADS = 4
MEM_WIDTH = MEM_HEADS * HEAD_DIM
REL_BUCKETS = 32
REL_MAX_DIST = 1024

OFF_DQ, OFF_DK, OFF_DV, OFF_DG = 0, 1024, 2048, 3072
OFF_LQ, OFF_LK, OFF_LV, OFF_LG = 4096, 5632, 7168, 8704
OFF_MQ, OFF_MG, OFF_MGATE = 9216, 9728, 10240
N_IN = 13312

LOG2E = 1.4426950408889634
LN2 = 0.6931471805599453

LANES = 128
MXU_EDGE = 256
VMEM_LIMIT_BYTES = 56 * 1024 * 1024

DIFF_TQ = 512
DIFF_TK = MXU_EDGE
DIL_TQ = 128
DIL_UNROLL = 4
FIN_T = 512
MM_TM = 2048
MM_SUB = 512
W_BLOCK = 512
STRIDE_STEP = 4


def _params(n_grid_dims):
    return pltpu.CompilerParams(dimension_semantics=("arbitrary",) * n_grid_dims,
                                vmem_limit_bytes=VMEM_LIMIT_BYTES)


def _dot_nt(a, b):
    return lax.dot_general(a, b, (((1,), (1,)), ((), ())), preferred_element_type=jnp.float32)


def _rms_kernel(x_ref, g_ref, o_ref):
    x = x_ref[...]
    ms = jnp.mean(x * x, axis=-1, keepdims=True)
    o_ref[...] = (x * lax.rsqrt(ms + EPS) * g_ref[...]).astype(o_ref.dtype)


def _rmsnorm(x2d, g, out_dtype, tm=512):
    m, d = x2d.shape
    tm = min(tm, m)
    return pl.pallas_call(
        _rms_kernel,
        grid=(m // tm,),
        in_specs=[pl.BlockSpec((tm, d), lambda i: (i, 0)),
                  pl.BlockSpec((1, d), lambda i: (0, 0))],
        out_specs=pl.BlockSpec((tm, d), lambda i: (i, 0)),
        out_shape=jax.ShapeDtypeStruct((m, d), out_dtype),
        compiler_params=_params(1),
        name="rmsnorm",
    )(x2d, g.reshape(1, d))


def _cast_weight_blocks(w_refs, s_refs, wbf_scr):
    for t, (w_ref, s_ref) in enumerate(zip(w_refs, s_refs)):
        wbf_scr[:, t * W_BLOCK:(t + 1) * W_BLOCK] = (w_ref[...] * s_ref[...]).astype(wbf_scr.dtype)


def _proj_kernel(a_ref, *refs, nb):
    w_refs, s_refs, o_ref, wbf_scr = refs[:nb], refs[nb:2 * nb], refs[2 * nb], refs[2 * nb + 1]

    @pl.when(pl.program_id(1) == 0)
    def _():
        _cast_weight_blocks(w_refs, s_refs, wbf_scr)

    sub = min(MM_SUB, a_ref.shape[0])

    def body(s, carry):
        r0 = pl.multiple_of(s * sub, sub)
        acc = jnp.dot(a_ref[pl.ds(r0, sub), :], wbf_scr[...], preferred_element_type=jnp.float32)
        o_ref[pl.ds(r0, sub), :] = acc.astype(o_ref.dtype)
        return carry
    lax.fori_loop(0, a_ref.shape[0] // sub, body, 0, unroll=True)


def _weight_specs(layer, blocks, n_grid):
    def spec(t, weight):
        if n_grid == 2:
            imap = (lambda j, i: (layer, 0, blocks(j, t))) if weight else (lambda j, i: (0, blocks(j, t)))
        else:
            imap = (lambda i: (layer, 0, blocks(0, t))) if weight else (lambda i: (0, blocks(0, t)))
        return pl.BlockSpec((None, D_MODEL, W_BLOCK) if weight else (1, W_BLOCK), imap)
    return spec


def _project(a, w_in, scales, layer, blocks, nb, n_tiles, out_dtype):
    m, k = a.shape
    tm = min(MM_TM, m)
    tn = nb * W_BLOCK
    spec = _weight_specs(layer, blocks, 2)
    return pl.pallas_call(
        functools.partial(_proj_kernel, nb=nb),
        grid=(n_tiles, m // tm),
        in_specs=([pl.BlockSpec((tm, k), lambda j, i: (i, 0))]
                  + [spec(t, True) for t in range(nb)] + [spec(t, False) for t in range(nb)]),
        out_specs=pl.BlockSpec((tm, tn), lambda j, i: (i, j)),
        out_shape=jax.ShapeDtypeStruct((m, n_tiles * tn), out_dtype),
        scratch_shapes=[pltpu.VMEM((k, tn), jnp.bfloat16)],
        compiler_params=_params(2),
        name="in_proj",
    )(a, *([w_in] * nb), *([scales] * nb))


def _proj_strided_kernel(a_ref, *refs, nb, r):
    w_refs, s_refs = refs[:nb], refs[nb:2 * nb]
    o_ref, wbf_scr = refs[2 * nb], refs[2 * nb + 1]
    scrs = refs[2 * nb + 2:]
    if r > STRIDE_STEP:
        scrs, tmp = scrs[:-1], scrs[-1]

    @pl.when(pl.program_id(0) == 0)
    def _():
        _cast_weight_blocks(w_refs, s_refs, wbf_scr)

    rows = a_ref.shape[0]
    n_slabs = wbf_scr.shape[1] // LANES
    sub = min(MM_SUB, rows)
    n_sub = rows // sub
    piece = sub // r

    def matmul_step(t):
        acc = jnp.dot(a_ref[t * sub:(t + 1) * sub, :], wbf_scr[...], preferred_element_type=jnp.float32)
        if r == 1:
            o_ref[0, t * sub:(t + 1) * sub, :] = acc.astype(o_ref.dtype)
        else:
            for k in range(n_slabs):
                scrs[t][k] = acc[:, k * LANES:(k + 1) * LANES]

    def relayout_step(t):
        out_rows = slice(t * piece, (t + 1) * piece)
        for k in range(n_slabs):
            cols = slice(k * LANES, (k + 1) * LANES)
            if r > STRIDE_STEP:
                r2 = r // STRIDE_STEP
                for c1 in range(STRIDE_STEP):
                    tmp[k, c1] = scrs[t][k, pl.ds(c1, sub // STRIDE_STEP, stride=STRIDE_STEP), :]
                for c1 in range(STRIDE_STEP):
                    for c2 in range(r2):
                        o_ref[c1 + STRIDE_STEP * c2, out_rows, cols] = (
                            tmp[k, c1, pl.ds(c2, piece, stride=r2), :].astype(o_ref.dtype))
            else:
                for c in range(r):
                    o_ref[c, out_rows, cols] = scrs[t][k, pl.ds(c, piece, stride=r), :].astype(o_ref.dtype)

    for t in range(n_sub + 1):
        if t < n_sub:
            matmul_step(t)
        if r > 1 and t > 0:
            relayout_step(t - 1)


def _project_strided(a, w_in, scales, layer, blocks, nb, batch, r, out_dtype):
    m, k = a.shape
    n = nb * W_BLOCK
    seq = m // batch
    spec = _weight_specs(layer, blocks, 1)
    return pl.pallas_call(
        functools.partial(_proj_strided_kernel, nb=nb, r=r),
        grid=(batch,),
        in_specs=([pl.BlockSpec((seq, k), lambda i: (i, 0))]
                  + [spec(t, True) for t in range(nb)] + [spec(t, False) for t in range(nb)]),
        out_specs=pl.BlockSpec((None, r, seq // r, n), lambda i: (i, 0, 0, 0)),
        out_shape=jax.ShapeDtypeStruct((batch, r, seq // r, n), out_dtype),
        scratch_shapes=([pltpu.VMEM((k, n), jnp.bfloat16)]
                        + ([pltpu.VMEM((n // LANES, min(MM_SUB, seq), LANES), jnp.float32)
                            for _ in range(seq // min(MM_SUB, seq))] if r > 1 else [])
                        + ([pltpu.VMEM((n // LANES, STRIDE_STEP, min(MM_SUB, seq) // STRIDE_STEP, LANES),
                                       jnp.float32)] if r > STRIDE_STEP else [])),
        compiler_params=_params(1),
        name=f"in_proj_stride{r}",
    )(a, *([w_in] * nb), *([scales] * nb))


def _mem_kv_kernel(mem_ref, g_ref, w_ref, o_ref):
    x = mem_ref[...]
    ms = jnp.mean(x * x, axis=-1, keepdims=True)
    mem_n = (x * lax.rsqrt(ms + EPS) * g_ref[...]).astype(jnp.bfloat16)
    o_ref[...] = jnp.dot(mem_n, w_ref[...].astype(jnp.bfloat16),
                         preferred_element_type=jnp.float32).astype(o_ref.dtype)


def _mem_kv(mem2d, g_mem, w_mem_kv):
    rows, d = mem2d.shape
    depth, _, n = w_mem_kv.shape
    return pl.pallas_call(
        _mem_kv_kernel,
        grid=(depth,),
        in_specs=[pl.BlockSpec((rows, d), lambda l: (0, 0)),
                  pl.BlockSpec((None, 1, d), lambda l: (l, 0, 0)),
                  pl.BlockSpec((None, d, n), lambda l: (l, 0, 0))],
        out_specs=pl.BlockSpec((None, rows, n), lambda l: (l, 0, 0)),
        out_shape=jax.ShapeDtypeStruct((depth, rows, n), jnp.bfloat16),
        compiler_params=_params(1),
        name="mem_kv",
    )(mem2d, g_mem.reshape(depth, 1, d), w_mem_kv)


def _t5_bucket(rel):
    half = REL_BUCKETS // 2
    max_exact = half // 2
    ret = jnp.where(rel > 0, half, 0)
    n = jnp.abs(rel)
    nf = jnp.maximum(n, 1).astype(jnp.float32)
    large = max_exact + (jnp.log(nf / max_exact) / math.log(REL_MAX_DIST / max_exact)
                         * (half - max_exact)).astype(jnp.int32)
    large = jnp.minimum(large, half - 1)
    return ret + jnp.where(n < max_exact, n, large)


def _toeplitz_kernel(u_ref, o_ref):
    n_tiles, rows, tile_w = o_ref.shape
    x = jnp.broadcast_to(u_ref[...], (rows, u_ref.shape[-1]))
    y = pltpu.roll(x, 0, 1, stride=1, stride_axis=0)
    for d in range(n_tiles):
        o_ref[d] = y[:, d * tile_w:(d + 1) * tile_w]


def _toeplitz_tiles(vals, rows, n_tiles, tile_w):
    groups = vals.shape[0]
    cols = n_tiles * tile_w
    period = pl.next_power_of_2(rows + cols - 1)
    pad = jnp.zeros((groups, period - (rows + cols - 1)), vals.dtype)
    u = jnp.concatenate([vals[:, rows - 1:], pad, vals[:, :rows - 1]], axis=1).reshape(groups, 1, period)
    return pl.pallas_call(
        _toeplitz_kernel,
        grid=(groups,),
        in_specs=[pl.BlockSpec((None, 1, period), lambda g: (g, 0, 0))],
        out_specs=pl.BlockSpec((None, n_tiles, rows, tile_w), lambda g: (g, 0, 0, 0)),
        out_shape=jax.ShapeDtypeStruct((groups, n_tiles, rows, tile_w), vals.dtype),
        compiler_params=_params(1),
        name="toeplitz_tiles",
    )(u)


def _diff_bias_tiles(rel_bias, seq):
    nd = seq // DIFF_TK - 1
    rel = jnp.arange(2 * seq - 1, dtype=jnp.int32) - (seq - 1)
    tvec = jnp.take(rel_bias[:, :DIFF_HEADS], _t5_bucket(rel), axis=0).T * LOG2E
    return _toeplitz_tiles(tvec, DIFF_TK, 2 * nd + 1, DIFF_TK)


def _dil_bias_tiles(rel_bias, g, seq):
    _, r = DIL_GROUPS[g]
    length = seq // r
    width = min(2 * DIL_TQ, length)
    shifts = jnp.array([0, -DIL_HALF, -(width - DIL_TQ)], dtype=jnp.int32)
    delta = shifts[:, None] + jnp.arange(DIL_TQ + width - 1, dtype=jnp.int32)[None, :] - (DIL_TQ - 1)
    c0 = DIFF_HEADS + g * DIL_HEADS
    bias = jnp.take(rel_bias[:, c0:c0 + DIL_HEADS], _t5_bucket(delta * r), axis=0) * LOG2E
    bias = jnp.where((jnp.abs(delta) <= DIL_HALF)[..., None], bias, NEG_INF)
    vals = jnp.transpose(bias, (0, 2, 1)).reshape(3 * DIL_HEADS, DIL_TQ + width - 1)
    return _toeplitz_tiles(vals, DIL_TQ, 1, width).reshape(3, DIL_HEADS, DIL_TQ, width)


def _diff_kernel(q0_ref, q1_ref, k0_ref, k1_ref, v_ref, bias_ref, lam_ref, o_ref,
                 s_scr, mx_scr, vext_scr, acc_scr, *, lam_init, nkc):
    i = pl.program_id(2)
    tq = q0_ref.shape[0]
    nrb = tq // DIFF_TK

    @pl.when(i == 0)
    def _():
        for a in range(2):
            vext_scr[a, :, 0:LANES] = v_ref[:, a * LANES:(a + 1) * LANES]
            vext_scr[a, :, LANES:2 * LANES] = jnp.ones((vext_scr.shape[1], LANES), vext_scr.dtype)

    lp = lam_ref[...]
    lam = (jnp.exp(jnp.sum(lp[0:1] * lp[1:2], axis=-1, keepdims=True))
           - jnp.exp(jnp.sum(lp[2:3] * lp[3:4], axis=-1, keepdims=True)) + lam_init)

    lane = lax.broadcasted_iota(jnp.int32, (tq, LANES), 1)
    q_refs = (q0_ref, q1_ref)
    k_refs = (k0_ref, k1_ref)

    def masked_q(u):
        a, m = divmod(u, 2)
        head_lanes = (lane >= a * DIFF_QK_DIM) & (lane < (a + 1) * DIFF_QK_DIM)
        q = q_refs[m][...]
        return jnp.where(head_lanes, q, jnp.zeros_like(q))

    def qk_chunk(u, qm, j):
        a, m = divmod(u, 2)
        s = _dot_nt(qm, k_refs[m][j * DIFF_TK:(j + 1) * DIFF_TK, :])
        for rb in range(nrb):
            rows = slice(rb * DIFF_TK, (rb + 1) * DIFF_TK)
            sb = s[rows] + bias_ref[a, j - nrb * i - rb + (nkc - 1)]
            s_scr[u, j, rows, :] = sb
            mtile = jnp.maximum(sb[:, :LANES], sb[:, LANES:])
            if j == 0:
                mx_scr[u, rows, :] = mtile
            else:
                mx_scr[u, rows, :] = jnp.maximum(mx_scr[u, rows, :], mtile)

    def row_max(u):
        return jnp.broadcast_to(jnp.max(mx_scr[u], axis=-1, keepdims=True), (tq, DIFF_TK))

    def pv_chunk(u, mb, j, acc):
        e = jnp.exp2(s_scr[u, j] - mb).astype(vext_scr.dtype)
        part = jnp.dot(e, vext_scr[u // 2, j * DIFF_TK:(j + 1) * DIFF_TK, :],
                       preferred_element_type=jnp.float32)
        return part if acc is None else acc + part

    def finish_head(a):
        n0 = acc_scr[2 * a]
        n1 = acc_scr[2 * a + 1]
        out = n0[:, :LANES] / n0[:, LANES:] - lam * (n1[:, :LANES] / n1[:, LANES:])
        ms = jnp.mean(out * out, axis=-1, keepdims=True)
        o_ref[:, a * LANES:(a + 1) * LANES] = (out * lax.rsqrt(ms + EPS) * (1.0 - lam_init)).astype(o_ref.dtype)

    n_units = 4
    mb_prev = None
    for u in range(n_units + 1):
        qm = masked_q(u) if u < n_units else None
        acc = None
        for j in range(nkc):
            if u < n_units:
                qk_chunk(u, qm, j)
            if u > 0:
                acc = pv_chunk(u - 1, mb_prev, j, acc)
        if u > 0:
            acc_scr[u - 1] = acc
            if (u - 1) % 2 == 1:
                finish_head((u - 1) // 2)
        if u < n_units:
            mb_prev = row_max(u)


def _diff_attention(zd, bias_tiles, lam_p, lam_init):
    b, s, _ = zd.shape
    tq = DIFF_TQ
    nkc = s // DIFF_TK
    kern = functools.partial(_diff_kernel, lam_init=lam_init, nkc=nkc)
    kblk = OFF_DK // LANES
    vblk = OFF_DV // (2 * LANES)
    return pl.pallas_call(
        kern,
        grid=(DIFF_HEADS // 2, b, s // tq),
        in_specs=[
            pl.BlockSpec((None, tq, LANES), lambda hp, bb, i: (bb, i, hp)),
            pl.BlockSpec((None, tq, LANES), lambda hp, bb, i: (bb, i, DIFF_HEADS // 2 + hp)),
            pl.BlockSpec((None, s, LANES), lambda hp, bb, i: (bb, 0, kblk + hp)),
            pl.BlockSpec((None, s, LANES), lambda hp, bb, i: (bb, 0, kblk + DIFF_HEADS // 2 + hp)),
            pl.BlockSpec((None, s, 2 * LANES), lambda hp, bb, i: (bb, 0, vblk + hp)),
            pl.BlockSpec((2, 2 * nkc - 1, DIFF_TK, DIFF_TK), lambda hp, bb, i: (hp, 0, 0, 0)),
            pl.BlockSpec((4, DIFF_QK_DIM), lambda hp, bb, i: (0, 0)),
        ],
        out_specs=pl.BlockSpec((None, tq, 2 * LANES), lambda hp, bb, i: (bb, i, hp)),
        out_shape=jax.ShapeDtypeStruct((b, s, DIFF_HEADS * DIFF_V_DIM), jnp.bfloat16),
        scratch_shapes=[
            pltpu.VMEM((4, nkc, tq, DIFF_TK), jnp.float32),
            pltpu.VMEM((4, tq, LANES), jnp.float32),
            pltpu.VMEM((2, s, 2 * LANES), jnp.bfloat16),
            pltpu.VMEM((4, tq, 2 * LANES), jnp.float32),
        ],
        compiler_params=_params(3),
        name="diff_attn",
    )(zd, zd, zd, zd, zd, bias_tiles, lam_p)


def _dil_kernel(z_ref, bias_ref, o_ref, lse_ref, *, r, length, width):
    nqb = length // DIL_TQ
    lane = lax.broadcasted_iota(jnp.int32, (DIL_TQ, LANES), 1)

    def block_coords(t):
        if nqb == 1:
            return t, 0
        if r == 1:
            return 0, t
        return t // nqb, t % nqb

    def body(tt, carry):
        staged = []
        for i in range(DIL_UNROLL):
            c, qb = block_coords(tt * DIL_UNROLL + i)
            q0 = pl.multiple_of(qb * DIL_TQ, DIL_TQ)
            ws = pl.multiple_of(jnp.clip(q0 - DIL_HALF, 0, length - width), DIL_HALF)
            var = jnp.where(qb == 0, 0, jnp.where(qb == nqb - 1, 2, 1))
            scs = []
            for h in range(DIL_HEADS):
                col = h * HEAD_DIM
                q = z_ref[c, pl.ds(q0, DIL_TQ), col:col + HEAD_DIM]
                kw = z_ref[c, pl.ds(ws, width), col + DIL_WIDTH:col + DIL_WIDTH + HEAD_DIM]
                scs.append(_dot_nt(q, kw) + bias_ref[var, h])
            staged.append((c, q0, ws, scs))
        for c, q0, ws, scs in staged:
            out_rows = pl.ds(q0 * r + c, DIL_TQ, stride=r) if r > 1 else pl.ds(q0, DIL_TQ)
            lse_tile = jnp.zeros((DIL_TQ, LANES), jnp.float32)
            for h in range(DIL_HEADS):
                col = 2 * DIL_WIDTH + h * HEAD_DIM
                vw = z_ref[c, pl.ds(ws, width), col:col + HEAD_DIM]
                sc = scs[h]
                mrow = jnp.max(sc, axis=-1, keepdims=True)
                e = jnp.exp2(sc - mrow)
                den = jnp.sum(e, axis=-1, keepdims=True)
                o_ref[h, out_rows, :] = jnp.dot(e.astype(vw.dtype), vw,
                                                preferred_element_type=jnp.float32) / den
                lse_tile = jnp.where(lane == h, mrow + jnp.log2(den), lse_tile)
            lse_ref[out_rows, :] = lse_tile
        return carry
    lax.fori_loop(0, r * nqb // DIL_UNROLL, body, 0)


def _dil_attention(zl, bias_tiles, g):
    b, r, length, zc = zl.shape
    s = length * r
    width = bias_tiles.shape[-1]
    kern = functools.partial(_dil_kernel, r=r, length=length, width=width)
    return pl.pallas_call(
        kern,
        grid=(b,),
        in_specs=[pl.BlockSpec((None, r, length, zc), lambda bb: (bb, 0, 0, 0)),
                  pl.BlockSpec(bias_tiles.shape, lambda bb: (0, 0, 0, 0))],
        out_specs=[pl.BlockSpec((None, DIL_HEADS, s, HEAD_DIM), lambda bb: (bb, 0, 0, 0)),
                   pl.BlockSpec((None, s, LANES), lambda bb: (bb, 0, 0))],
        out_shape=[jax.ShapeDtypeStruct((b, DIL_HEADS, s, HEAD_DIM), jnp.float32),
                   jax.ShapeDtypeStruct((b, s, LANES), jnp.float32)],
        compiler_params=_params(1),
        name=f"dil_attn_g{g}",
    )(zl, bias_tiles)


def _finish_kernel(x_ref, oa_ref, ob0_ref, ob1_ref, ob2_ref, l0_ref, l1_ref, l2_ref, dg_ref, zg_ref,
                   mkv_ref, wa_ref, wb_ref, wm_ref, wo_ref, g_ref, *out_refs, final):
    bf16 = jnp.bfloat16
    f32 = jnp.float32

    def half_silu(zh):
        zh = zh.astype(f32)
        return zh + zh * jnp.tanh(zh)

    c_lg, c_mq, c_mg, c_gate = 0, DIL_WIDTH, DIL_WIDTH + MEM_WIDTH, DIL_WIDTH + 2 * MEM_WIDTH

    ya = jnp.dot((oa_ref[...].astype(f32) * half_silu(dg_ref[...])).astype(bf16), wa_ref[...],
                 preferred_element_type=jnp.float32)

    l0, l1, l2 = l0_ref[...], l1_ref[...], l2_ref[...]
    lmax = jnp.maximum(jnp.maximum(l0, l1), l2)
    w0, w1, w2 = jnp.exp2(l0 - lmax), jnp.exp2(l1 - lmax), jnp.exp2(l2 - lmax)
    inv = 1.0 / (w0 + w1 + w2)
    w0, w1, w2 = w0 * inv, w1 * inv, w2 * inv
    ob_refs = (ob0_ref, ob1_ref, ob2_ref)
    parts = []
    for h in range(DIL_HEADS):
        acc = None
        for wg, ob_ref in zip((w0, w1, w2), ob_refs):
            term = wg[:, h:h + 1] * ob_ref[h]
            acc = term if acc is None else acc + term
        parts.append(acc)
    ob = jnp.concatenate(parts, axis=-1)
    yb = jnp.dot((ob * half_silu(zg_ref[:, c_lg:c_lg + DIL_WIDTH])).astype(bf16), wb_ref[...],
                 preferred_element_type=jnp.float32)

    parts = []
    for h in range(MEM_HEADS):
        cols = slice(h * HEAD_DIM, (h + 1) * HEAD_DIM)
        sc = _dot_nt(zg_ref[:, c_mq + h * HEAD_DIM:c_mq + (h + 1) * HEAD_DIM], mkv_ref[:, cols])
        e = jnp.exp2(sc - jnp.max(sc, axis=-1, keepdims=True))
        den = jnp.sum(e, axis=-1, keepdims=True)
        vcols = slice(MEM_WIDTH + h * HEAD_DIM, MEM_WIDTH + (h + 1) * HEAD_DIM)
        parts.append(jnp.dot(e.astype(bf16), mkv_ref[:, vcols], preferred_element_type=jnp.float32) / den)
    om = jnp.concatenate(parts, axis=-1)
    ym = jnp.dot((om * half_silu(zg_ref[:, c_mg:c_mg + MEM_WIDTH])).astype(bf16), wm_ref[...],
                 preferred_element_type=jnp.float32)

    merged = (ya + yb + ym
              + jnp.tanh(zg_ref[:, c_gate:c_gate + D_MODEL].astype(f32)) * ya
              + jnp.tanh(zg_ref[:, c_gate + D_MODEL:c_gate + 2 * D_MODEL].astype(f32)) * yb
              + jnp.tanh(zg_ref[:, c_gate + 2 * D_MODEL:c_gate + 3 * D_MODEL].astype(f32)) * ym)
    xn = x_ref[...] + jnp.dot(merged.astype(bf16), wo_ref[...], preferred_element_type=jnp.float32)
    hn = xn * lax.rsqrt(jnp.mean(xn * xn, axis=-1, keepdims=True) + EPS) * g_ref[...]
    if final:
        out_refs[0][...] = hn
    else:
        out_refs[0][...] = xn
        out_refs[1][...] = hn.astype(out_refs[1].dtype)


def _finish(x, oa, obs, lses, zd, zg, mkv, wa, wb, wm, wo, g_next, final):
    b, s, d = x.shape
    t = FIN_T
    row = lambda width: pl.BlockSpec((None, t, width), lambda bb, i: (bb, i, 0))
    full = lambda arr: pl.BlockSpec(arr.shape, lambda bb, i: (0,) * arr.ndim)
    heads = pl.BlockSpec((None, DIL_HEADS, t, HEAD_DIM), lambda bb, i: (bb, 0, i, 0))
    in_specs = [row(d), row(d), heads, heads, heads,
                row(LANES), row(LANES), row(LANES),
                pl.BlockSpec((None, t, d), lambda bb, i: (bb, i, OFF_DG // D_MODEL)),
                row(zg.shape[-1]),
                pl.BlockSpec((None, N_MEM, 2 * MEM_WIDTH), lambda bb, i: (bb, 0, 0)),
                full(wa), full(wb), full(wm), full(wo),
                pl.BlockSpec((1, d), lambda bb, i: (0, 0))]
    if final:
        out_specs = [row(d)]
        out_shape = [jax.ShapeDtypeStruct((b, s, d), jnp.float32)]
    else:
        out_specs = [row(d), row(d)]
        out_shape = [jax.ShapeDtypeStruct((b, s, d), jnp.float32),
                     jax.ShapeDtypeStruct((b, s, d), jnp.bfloat16)]
    return pl.pallas_call(
        functools.partial(_finish_kernel, final=final),
        grid=(b, s // t),
        in_specs=in_specs,
        out_specs=out_specs,
        out_shape=out_shape,
        compiler_params=_params(2),
        name="finish",
    )(x, oa, *obs, *lses, zd, zg, mkv, wa, wb, wm, wo, g_next.reshape(1, d))


def kernel(x, mem, g_norm, w_in, diff_lambda, w_mem_kv, g_mem, w_br_diff, w_br_dil, w_br_mem, w_out,
           rel_bias, g_final):
    b, s, d = x.shape
    depth = w_in.shape[0]
    bf16 = jnp.bfloat16
    m_rows = b * s

    diff_tiles = _diff_bias_tiles(rel_bias, s)
    dil_tiles = [_dil_bias_tiles(rel_bias, g, s) for g in range(len(DIL_GROUPS))]
    mkv_all = _mem_kv(mem.reshape(b * N_MEM, d), g_mem, w_mem_kv)

    col_scale = np.ones((1, N_IN), np.float32)
    col_scale[0, OFF_DQ:OFF_DK] = DIFF_QK_DIM ** -0.5 * LOG2E
    col_scale[0, OFF_LQ:OFF_LK] = HEAD_DIM ** -0.5 * LOG2E
    col_scale[0, OFF_MQ:OFF_MG] = HEAD_DIM ** -0.5 * LOG2E
    col_scale[0, OFF_DG:OFF_LQ] = 0.5
    col_scale[0, OFF_LG:OFF_MQ] = 0.5
    col_scale[0, OFF_MG:] = 0.5
    col_scale = jnp.asarray(col_scale)

    h = _rmsnorm(x.reshape(m_rows, d), g_norm[0], bf16)
    out = None
    for l in range(depth):
        zd = _project(h, w_in, col_scale, l, lambda j, t: 4 * j + t, 4, OFF_LQ // (4 * W_BLOCK),
                      bf16).reshape(b, s, -1)
        zg = _project(h, w_in, col_scale, l, lambda j, t: OFF_LG // W_BLOCK + 3 * j + t, 3,
                      (N_IN - OFF_LG) // (3 * W_BLOCK), bf16).reshape(b, s, -1)
        zls = [_project_strided(h, w_in, col_scale, l,
                                lambda j, t, g=g: (OFF_LQ, OFF_LK, OFF_LV)[t] // W_BLOCK + g, 3,
                                b, DIL_GROUPS[g][1], bf16) for g in range(len(DIL_GROUPS))]

        mkv = mkv_all[l].reshape(b, N_MEM, 2 * MEM_WIDTH)
        lam_init = 0.8 - 0.6 * math.exp(-0.3 * l)
        oa = _diff_attention(zd, diff_tiles, diff_lambda[l], lam_init)
        obs, lses = zip(*[_dil_attention(zls[g], dil_tiles[g], g) for g in range(len(DIL_GROUPS))])

        final = l == depth - 1
        g_next = g_final if final else g_norm[l + 1]
        res = _finish(x, oa, obs, lses, zd, zg, mkv, w_br_diff[l].astype(bf16), w_br_dil[l].astype(bf16),
                      w_br_mem[l].astype(bf16), (0.5 * w_out[l]).astype(bf16), g_next, final)
        if final:
            out = res[0]
        else:
            x, h3 = res
            h = h3.reshape(m_rows, d)
    return out
```

```python
import functools
import math

import jax
import jax.numpy as jnp
import numpy as np
from jax import lax
from jax.experimental import pallas as pl
from jax.experimental.pallas import tpu as pltpu

D_MODEL = 1024
N_MEM = 256
EPS = 1e-6
NEG_INF = -1e30

DIFF_HEADS = 8
DIFF_QK_DIM = 64
DIFF_V_DIM = 128
DIL_GROUPS = ((128, 1), (512, 4), (2048, 16))
DIL_HEADS = 4
HEAD_DIM = 128
DIL_WIDTH = DIL_HEADS * HEAD_DIM
DIL_HALF = 64
MEM_HEADS = 4
MEM_WIDTH = MEM_HEADS * HEAD_DIM
REL_BUCKETS = 32
REL_MAX_DIST = 1024

OFF_DQ, OFF_DK, OFF_DV, OFF_DG = 0, 1024, 2048, 3072
OFF_LQ, OFF_LK, OFF_LV, OFF_LG = 4096, 5632, 7168, 8704
OFF_MQ, OFF_MG, OFF_MGATE = 9216, 9728, 10240
N_IN = 13312

LOG2E = 1.4426950408889634
LN2 = 0.6931471805599453

LANES = 128
MXU_EDGE = 256
VMEM_LIMIT_BYTES = 56 * 1024 * 1024

DIFF_TQ = 512
DIFF_TK = MXU_EDGE
DIL_TQ = 128
DIL_UNROLL = 4
FIN_T = 512
MM_TM = 2048
MM_SUB = 512
W_BLOCK = 512
STRIDE_STEP = 4


def _params(n_grid_dims):
    return pltpu.CompilerParams(dimension_semantics=("arbitrary",) * n_grid_dims,
                                vmem_limit_bytes=VMEM_LIMIT_BYTES)


def _dot_nt(a, b):
    return lax.dot_general(a, b, (((1,), (1,)), ((), ())), preferred_element_type=jnp.float32)


def _rms_kernel(x_ref, g_ref, o_ref):
    x = x_ref[...]
    ms = jnp.mean(x * x, axis=-1, keepdims=True)
    o_ref[...] = (x * lax.rsqrt(ms + EPS) * g_ref[...]).astype(o_ref.dtype)


def _rmsnorm(x2d, g, out_dtype, tm=512):
    m, d = x2d.shape
    tm = min(tm, m)
    return pl.pallas_call(
        _rms_kernel,
        grid=(m // tm,),
        in_specs=[pl.BlockSpec((tm, d), lambda i: (i, 0)),
                  pl.BlockSpec((1, d), lambda i: (0, 0))],
        out_specs=pl.BlockSpec((tm, d), lambda i: (i, 0)),
        out_shape=jax.ShapeDtypeStruct((m, d), out_dtype),
        compiler_params=_params(1),
        name="rmsnorm",
    )(x2d, g.reshape(1, d))


def _cast_weight_blocks(w_refs, s_refs, wbf_scr):
    for t, (w_ref, s_ref) in enumerate(zip(w_refs, s_refs)):
        wbf_scr[:, t * W_BLOCK:(t + 1) * W_BLOCK] = (w_ref[...] * s_ref[...]).astype(wbf_scr.dtype)


def _proj_kernel(a_ref, *refs, nb):
    w_refs, s_refs, o_ref, wbf_scr = refs[:nb], refs[nb:2 * nb], refs[2 * nb], refs[2 * nb + 1]

    @pl.when(pl.program_id(1) == 0)
    def _():
        _cast_weight_blocks(w_refs, s_refs, wbf_scr)

    sub = min(MM_SUB, a_ref.shape[0])

    def body(s, carry):
        r0 = pl.multiple_of(s * sub, sub)
        acc = jnp.dot(a_ref[pl.ds(r0, sub), :], wbf_scr[...], preferred_element_type=jnp.float32)
        o_ref[pl.ds(r0, sub), :] = acc.astype(o_ref.dtype)
        return carry
    lax.fori_loop(0, a_ref.shape[0] // sub, body, 0, unroll=True)


def _weight_specs(layer, blocks, n_grid):
    def spec(t, weight):
        if n_grid == 2:
            imap = (lambda j, i: (layer, 0, blocks(j, t))) if weight else (lambda j, i: (0, blocks(j, t)))
        else:
            imap = (lambda i: (layer, 0, blocks(0, t))) if weight else (lambda i: (0, blocks(0, t)))
        return pl.BlockSpec((None, D_MODEL, W_BLOCK) if weight else (1, W_BLOCK), imap)
    return spec


def _project(a, w_in, scales, layer, blocks, nb, n_tiles, out_dtype):
    m, k = a.shape
    tm = min(MM_TM, m)
    tn = nb * W_BLOCK
    spec = _weight_specs(layer, blocks, 2)
    return pl.pallas_call(
        functools.partial(_proj_kernel, nb=nb),
        grid=(n_tiles, m // tm),
        in_specs=([pl.BlockSpec((tm, k), lambda j, i: (i, 0))]
                  + [spec(t, True) for t in range(nb)] + [spec(t, False) for t in range(nb)]),
        out_specs=pl.BlockSpec((tm, tn), lambda j, i: (i, j)),
        out_shape=jax.ShapeDtypeStruct((m, n_tiles * tn), out_dtype),
        scratch_shapes=[pltpu.VMEM((k, tn), jnp.bfloat16)],
        compiler_params=_params(2),
        name="in_proj",
    )(a, *([w_in] * nb), *([scales] * nb))


def _project_by_residue(a_ref, wbf_scr, o_ref, scrs, r):
    if r > STRIDE_STEP:
        scrs, tmp = scrs[:-1], scrs[-1]

    rows = a_ref.shape[0]
    n_slabs = wbf_scr.shape[1] // LANES
    sub = min(MM_SUB, rows)
    n_sub = rows // sub
    piece = sub // r

    def matmul_step(t):
        acc = jnp.dot(a_ref[t * sub:(t + 1) * sub, :], wbf_scr[...], preferred_element_type=jnp.float32)
        if r == 1:
            o_ref[0, t * sub:(t + 1) * sub, :] = acc.astype(o_ref.dtype)
        else:
            for k in range(n_slabs):
                scrs[t][k] = acc[:, k * LANES:(k + 1) * LANES]

    def relayout_step(t):
        out_rows = slice(t * piece, (t + 1) * piece)
        for k in range(n_slabs):
            cols = slice(k * LANES, (k + 1) * LANES)
            if r > STRIDE_STEP:
                r2 = r // STRIDE_STEP
                for c1 in range(STRIDE_STEP):
                    tmp[k, c1] = scrs[t][k, pl.ds(c1, sub // STRIDE_STEP, stride=STRIDE_STEP), :]
                for c1 in range(STRIDE_STEP):
                    for c2 in range(r2):
                        o_ref[c1 + STRIDE_STEP * c2, out_rows, cols] = (
                            tmp[k, c1, pl.ds(c2, piece, stride=r2), :].astype(o_ref.dtype))
            else:
                for c in range(r):
                    o_ref[c, out_rows, cols] = scrs[t][k, pl.ds(c, piece, stride=r), :].astype(o_ref.dtype)

    for t in range(n_sub + 1):
        if t < n_sub:
            matmul_step(t)
        if r > 1 and t > 0:
            relayout_step(t - 1)


def _residue_scratch(n, seq, r):
    sub = min(MM_SUB, seq)
    return (([pltpu.VMEM((n // LANES, sub, LANES), jnp.float32) for _ in range(seq // sub)] if r > 1 else [])
            + ([pltpu.VMEM((n // LANES, STRIDE_STEP, sub // STRIDE_STEP, LANES), jnp.float32)]
               if r > STRIDE_STEP else []))


def _mem_kv_kernel(mem_ref, g_ref, w_ref, o_ref):
    x = mem_ref[...]
    ms = jnp.mean(x * x, axis=-1, keepdims=True)
    mem_n = (x * lax.rsqrt(ms + EPS) * g_ref[...]).astype(jnp.bfloat16)
    o_ref[...] = jnp.dot(mem_n, w_ref[...].astype(jnp.bfloat16),
                         preferred_element_type=jnp.float32).astype(o_ref.dtype)


def _mem_kv(mem2d, g_mem, w_mem_kv):
    rows, d = mem2d.shape
    depth, _, n = w_mem_kv.shape
    return pl.pallas_call(
        _mem_kv_kernel,
        grid=(depth,),
        in_specs=[pl.BlockSpec((rows, d), lambda l: (0, 0)),
                  pl.BlockSpec((None, 1, d), lambda l: (l, 0, 0)),
                  pl.BlockSpec((None, d, n), lambda l: (l, 0, 0))],
        out_specs=pl.BlockSpec((None, rows, n), lambda l: (l, 0, 0)),
        out_shape=jax.ShapeDtypeStruct((depth, rows, n), jnp.bfloat16),
        compiler_params=_params(1),
        name="mem_kv",
    )(mem2d, g_mem.reshape(depth, 1, d), w_mem_kv)


def _t5_bucket(rel):
    half = REL_BUCKETS // 2
    max_exact = half // 2
    ret = jnp.where(rel > 0, half, 0)
    n = jnp.abs(rel)
    nf = jnp.maximum(n, 1).astype(jnp.float32)
    large = max_exact + (jnp.log(nf / max_exact) / math.log(REL_MAX_DIST / max_exact)
                         * (half - max_exact)).astype(jnp.int32)
    large = jnp.minimum(large, half - 1)
    return ret + jnp.where(n < max_exact, n, large)


def _toeplitz_kernel(u_ref, o_ref):
    n_tiles, rows, tile_w = o_ref.shape
    x = jnp.broadcast_to(u_ref[...], (rows, u_ref.shape[-1]))
    y = pltpu.roll(x, 0, 1, stride=1, stride_axis=0)
    for d in range(n_tiles):
        o_ref[d] = y[:, d * tile_w:(d + 1) * tile_w]


def _toeplitz_tiles(vals, rows, n_tiles, tile_w):
    groups = vals.shape[0]
    cols = n_tiles * tile_w
    period = pl.next_power_of_2(rows + cols - 1)
    pad = jnp.zeros((groups, period - (rows + cols - 1)), vals.dtype)
    u = jnp.concatenate([vals[:, rows - 1:], pad, vals[:, :rows - 1]], axis=1).reshape(groups, 1, period)
    return pl.pallas_call(
        _toeplitz_kernel,
        grid=(groups,),
        in_specs=[pl.BlockSpec((None, 1, period), lambda g: (g, 0, 0))],
        out_specs=pl.BlockSpec((None, n_tiles, rows, tile_w), lambda g: (g, 0, 0, 0)),
        out_shape=jax.ShapeDtypeStruct((groups, n_tiles, rows, tile_w), vals.dtype),
        compiler_params=_params(1),
        name="toeplitz_tiles",
    )(u)


def _diff_bias_tiles(rel_bias, seq):
    nd = seq // DIFF_TK - 1
    rel = jnp.arange(2 * seq - 1, dtype=jnp.int32) - (seq - 1)
    tvec = jnp.take(rel_bias[:, :DIFF_HEADS], _t5_bucket(rel), axis=0).T * LOG2E
    return _toeplitz_tiles(tvec, DIFF_TK, 2 * nd + 1, DIFF_TK)


def _dil_bias_tiles(rel_bias, g, seq):
    _, r = DIL_GROUPS[g]
    length = seq // r
    width = min(2 * DIL_TQ, length)
    shifts = jnp.array([0, -DIL_HALF, -(width - DIL_TQ)], dtype=jnp.int32)
    delta = shifts[:, None] + jnp.arange(DIL_TQ + width - 1, dtype=jnp.int32)[None, :] - (DIL_TQ - 1)
    c0 = DIFF_HEADS + g * DIL_HEADS
    bias = jnp.take(rel_bias[:, c0:c0 + DIL_HEADS], _t5_bucket(delta * r), axis=0) * LOG2E
    bias = jnp.where((jnp.abs(delta) <= DIL_HALF)[..., None], bias, NEG_INF)
    vals = jnp.transpose(bias, (0, 2, 1)).reshape(3 * DIL_HEADS, DIL_TQ + width - 1)
    return _toeplitz_tiles(vals, DIL_TQ, 1, width).reshape(3, DIL_HEADS, DIL_TQ, width)


def _diff_kernel(q0_ref, q1_ref, k0_ref, k1_ref, v_ref, bias_ref, lam_ref, o_ref,
                 s_scr, mx_scr, vext_scr, acc_scr, *, lam_init, nkc):
    i = pl.program_id(2)
    tq = q0_ref.shape[0]
    nrb = tq // DIFF_TK

    @pl.when(i == 0)
    def _():
        for a in range(2):
            vext_scr[a, :, 0:LANES] = v_ref[:, a * LANES:(a + 1) * LANES]
            vext_scr[a, :, LANES:2 * LANES] = jnp.ones((vext_scr.shape[1], LANES), vext_scr.dtype)

    lp = lam_ref[...]
    lam = (jnp.exp(jnp.sum(lp[0:1] * lp[1:2], axis=-1, keepdims=True))
           - jnp.exp(jnp.sum(lp[2:3] * lp[3:4], axis=-1, keepdims=True)) + lam_init)

    lane = lax.broadcasted_iota(jnp.int32, (tq, LANES), 1)
    q_refs = (q0_ref, q1_ref)
    k_refs = (k0_ref, k1_ref)

    def masked_q(u):
        a, m = divmod(u, 2)
        head_lanes = (lane >= a * DIFF_QK_DIM) & (lane < (a + 1) * DIFF_QK_DIM)
        q = q_refs[m][...]
        return jnp.where(head_lanes, q, jnp.zeros_like(q))

    def qk_chunk(u, qm, j):
        a, m = divmod(u, 2)
        s = _dot_nt(qm, k_refs[m][j * DIFF_TK:(j + 1) * DIFF_TK, :])
        for rb in range(nrb):
            rows = slice(rb * DIFF_TK, (rb + 1) * DIFF_TK)
            sb = s[rows] + bias_ref[a, j - nrb * i - rb + (nkc - 1)]
            s_scr[u, j, rows, :] = sb
            mtile = jnp.maximum(sb[:, :LANES], sb[:, LANES:])
            if j == 0:
                mx_scr[u, rows, :] = mtile
            else:
                mx_scr[u, rows, :] = jnp.maximum(mx_scr[u, rows, :], mtile)

    def row_max(u):
        return jnp.broadcast_to(jnp.max(mx_scr[u], axis=-1, keepdims=True), (tq, DIFF_TK))

    def pv_chunk(u, mb, j, acc):
        e = jnp.exp2(s_scr[u, j] - mb).astype(vext_scr.dtype)
        part = jnp.dot(e, vext_scr[u // 2, j * DIFF_TK:(j + 1) * DIFF_TK, :],
                       preferred_element_type=jnp.float32)
        return part if acc is None else acc + part

    def finish_head(a):
        n0 = acc_scr[2 * a]
        n1 = acc_scr[2 * a + 1]
        out = n0[:, :LANES] / n0[:, LANES:] - lam * (n1[:, :LANES] / n1[:, LANES:])
        ms = jnp.mean(out * out, axis=-1, keepdims=True)
        o_ref[:, a * LANES:(a + 1) * LANES] = (out * lax.rsqrt(ms + EPS) * (1.0 - lam_init)).astype(o_ref.dtype)

    n_units = 4
    mb_prev = None
    for u in range(n_units + 1):
        qm = masked_q(u) if u < n_units else None
        acc = None
        for j in range(nkc):
            if u < n_units:
                qk_chunk(u, qm, j)
            if u > 0:
                acc = pv_chunk(u - 1, mb_prev, j, acc)
        if u > 0:
            acc_scr[u - 1] = acc
            if (u - 1) % 2 == 1:
                finish_head((u - 1) // 2)
        if u < n_units:
            mb_prev = row_max(u)


def _diff_attention(zd, bias_tiles, lam_p, lam_init):
    b, s, _ = zd.shape
    tq = DIFF_TQ
    nkc = s // DIFF_TK
    kern = functools.partial(_diff_kernel, lam_init=lam_init, nkc=nkc)
    kblk = OFF_DK // LANES
    vblk = OFF_DV // (2 * LANES)
    return pl.pallas_call(
        kern,
        grid=(DIFF_HEADS // 2, b, s // tq),
        in_specs=[
            pl.BlockSpec((None, tq, LANES), lambda hp, bb, i: (bb, i, hp)),
            pl.BlockSpec((None, tq, LANES), lambda hp, bb, i: (bb, i, DIFF_HEADS // 2 + hp)),
            pl.BlockSpec((None, s, LANES), lambda hp, bb, i: (bb, 0, kblk + hp)),
            pl.BlockSpec((None, s, LANES), lambda hp, bb, i: (bb, 0, kblk + DIFF_HEADS // 2 + hp)),
            pl.BlockSpec((None, s, 2 * LANES), lambda hp, bb, i: (bb, 0, vblk + hp)),
            pl.BlockSpec((2, 2 * nkc - 1, DIFF_TK, DIFF_TK), lambda hp, bb, i: (hp, 0, 0, 0)),
            pl.BlockSpec((4, DIFF_QK_DIM), lambda hp, bb, i: (0, 0)),
        ],
        out_specs=pl.BlockSpec((None, tq, 2 * LANES), lambda hp, bb, i: (bb, i, hp)),
        out_shape=jax.ShapeDtypeStruct((b, s, DIFF_HEADS * DIFF_V_DIM), jnp.bfloat16),
        scratch_shapes=[
            pltpu.VMEM((4, nkc, tq, DIFF_TK), jnp.float32),
            pltpu.VMEM((4, tq, LANES), jnp.float32),
            pltpu.VMEM((2, s, 2 * LANES), jnp.bfloat16),
            pltpu.VMEM((4, tq, 2 * LANES), jnp.float32),
        ],
        compiler_params=_params(3),
        name="diff_attn",
    )(zd, zd, zd, zd, zd, bias_tiles, lam_p)


def _dil_kernel(a_ref, *refs, nb, r, length, width):
    w_refs, s_refs = refs[:nb], refs[nb:2 * nb]
    bias_ref, o_ref, lse_ref, wbf_scr, z_ref = refs[2 * nb:2 * nb + 5]

    @pl.when(pl.program_id(0) == 0)
    def _():
        _cast_weight_blocks(w_refs, s_refs, wbf_scr)

    _project_by_residue(a_ref, wbf_scr, z_ref, refs[2 * nb + 5:], r)

    nqb = length // DIL_TQ
    lane = lax.broadcasted_iota(jnp.int32, (DIL_TQ, LANES), 1)

    def block_coords(t):
        if nqb == 1:
            return t, 0
        if r == 1:
            return 0, t
        return t // nqb, t % nqb

    def body(tt, carry):
        staged = []
        for i in range(DIL_UNROLL):
            c, qb = block_coords(tt * DIL_UNROLL + i)
            q0 = pl.multiple_of(qb * DIL_TQ, DIL_TQ)
            ws = pl.multiple_of(jnp.clip(q0 - DIL_HALF, 0, length - width), DIL_HALF)
            var = jnp.where(qb == 0, 0, jnp.where(qb == nqb - 1, 2, 1))
            scs = []
            for h in range(DIL_HEADS):
                col = h * HEAD_DIM
                q = z_ref[c, pl.ds(q0, DIL_TQ), col:col + HEAD_DIM]
                kw = z_ref[c, pl.ds(ws, width), col + DIL_WIDTH:col + DIL_WIDTH + HEAD_DIM]
                scs.append(_dot_nt(q, kw) + bias_ref[var, h])
            staged.append((c, q0, ws, scs))
        for c, q0, ws, scs in staged:
            out_rows = pl.ds(q0 * r + c, DIL_TQ, stride=r) if r > 1 else pl.ds(q0, DIL_TQ)
            lse_tile = jnp.zeros((DIL_TQ, LANES), jnp.float32)
            for h in range(DIL_HEADS):
                col = 2 * DIL_WIDTH + h * HEAD_DIM
                vw = z_ref[c, pl.ds(ws, width), col:col + HEAD_DIM]
                sc = scs[h]
                mrow = jnp.max(sc, axis=-1, keepdims=True)
                e = jnp.exp2(sc - mrow)
                den = jnp.sum(e, axis=-1, keepdims=True)
                o_ref[h, out_rows, :] = jnp.dot(e.astype(vw.dtype), vw,
                                                preferred_element_type=jnp.float32) / den
                lse_tile = jnp.where(lane == h, mrow + jnp.log2(den), lse_tile)
            lse_ref[out_rows, :] = lse_tile
        return carry
    lax.fori_loop(0, r * nqb // DIL_UNROLL, body, 0)


def _dil_attention(h, w_in, scales, layer, g, bias_tiles, batch):
    m, k = h.shape
    s = m // batch
    _, r = DIL_GROUPS[g]
    length = s // r
    nb = 3
    n = nb * W_BLOCK
    width = bias_tiles.shape[-1]
    spec = _weight_specs(layer, lambda j, t: (OFF_LQ, OFF_LK, OFF_LV)[t] // W_BLOCK + g, 1)
    kern = functools.partial(_dil_kernel, nb=nb, r=r, length=length, width=width)
    return pl.pallas_call(
        kern,
        grid=(batch,),
        in_specs=([pl.BlockSpec((s, k), lambda bb: (bb, 0))]
                  + [spec(t, True) for t in range(nb)] + [spec(t, False) for t in range(nb)]
                  + [pl.BlockSpec(bias_tiles.shape, lambda bb: (0, 0, 0, 0))]),
        out_specs=[pl.BlockSpec((None, DIL_HEADS, s, HEAD_DIM), lambda bb: (bb, 0, 0, 0)),
                   pl.BlockSpec((None, s, LANES), lambda bb: (bb, 0, 0))],
        out_shape=[jax.ShapeDtypeStruct((batch, DIL_HEADS, s, HEAD_DIM), jnp.float32),
                   jax.ShapeDtypeStruct((batch, s, LANES), jnp.float32)],
        scratch_shapes=([pltpu.VMEM((k, n), jnp.bfloat16), pltpu.VMEM((r, length, n), jnp.bfloat16)]
                        + _residue_scratch(n, s, r)),
        compiler_params=_params(1),
        name=f"dil_attn_g{g}",
    )(h, *([w_in] * nb), *([scales] * nb), bias_tiles)


def _finish_kernel(x_ref, oa_ref, ob0_ref, ob1_ref, ob2_ref, l0_ref, l1_ref, l2_ref, dg_ref, zg_ref,
                   mkv_ref, wa_ref, wb_ref, wm_ref, wo_ref, g_ref, *out_refs, final):
    bf16 = jnp.bfloat16
    f32 = jnp.float32

    def half_silu(zh):
        zh = zh.astype(f32)
        return zh + zh * jnp.tanh(zh)

    c_lg, c_mq, c_mg, c_gate = 0, DIL_WIDTH, DIL_WIDTH + MEM_WIDTH, DIL_WIDTH + 2 * MEM_WIDTH

    ya = jnp.dot((oa_ref[...].astype(f32) * half_silu(dg_ref[...])).astype(bf16), wa_ref[...],
                 preferred_element_type=jnp.float32)

    l0, l1, l2 = l0_ref[...], l1_ref[...], l2_ref[...]
    lmax = jnp.maximum(jnp.maximum(l0, l1), l2)
    w0, w1, w2 = jnp.exp2(l0 - lmax), jnp.exp2(l1 - lmax), jnp.exp2(l2 - lmax)
    inv = 1.0 / (w0 + w1 + w2)
    w0, w1, w2 = w0 * inv, w1 * inv, w2 * inv
    ob_refs = (ob0_ref, ob1_ref, ob2_ref)
    parts = []
    for h in range(DIL_HEADS):
        acc = None
        for wg, ob_ref in zip((w0, w1, w2), ob_refs):
            term = wg[:, h:h + 1] * ob_ref[h]
            acc = term if acc is None else acc + term
        parts.append(acc)
    ob = jnp.concatenate(parts, axis=-1)
    yb = jnp.dot((ob * half_silu(zg_ref[:, c_lg:c_lg + DIL_WIDTH])).astype(bf16), wb_ref[...],
                 preferred_element_type=jnp.float32)

    parts = []
    for h in range(MEM_HEADS):
        cols = slice(h * HEAD_DIM, (h + 1) * HEAD_DIM)
        sc = _dot_nt(zg_ref[:, c_mq + h * HEAD_DIM:c_mq + (h + 1) * HEAD_DIM], mkv_ref[:, cols])
        e = jnp.exp2(sc - jnp.max(sc, axis=-1, keepdims=True))
        den = jnp.sum(e, axis=-1, keepdims=True)
        vcols = slice(MEM_WIDTH + h * HEAD_DIM, MEM_WIDTH + (h + 1) * HEAD_DIM)
        parts.append(jnp.dot(e.astype(bf16), mkv_ref[:, vcols], preferred_element_type=jnp.float32) / den)
    om = jnp.concatenate(parts, axis=-1)
    ym = jnp.dot((om * half_silu(zg_ref[:, c_mg:c_mg + MEM_WIDTH])).astype(bf16), wm_ref[...],
                 preferred_element_type=jnp.float32)

    merged = (ya + yb + ym
              + jnp.tanh(zg_ref[:, c_gate:c_gate + D_MODEL].astype(f32)) * ya
              + jnp.tanh(zg_ref[:, c_gate + D_MODEL:c_gate + 2 * D_MODEL].astype(f32)) * yb
              + jnp.tanh(zg_ref[:, c_gate + 2 * D_MODEL:c_gate + 3 * D_MODEL].astype(f32)) * ym)
    xn = x_ref[...] + jnp.dot(merged.astype(bf16), wo_ref[...], preferred_element_type=jnp.float32)
    hn = xn * lax.rsqrt(jnp.mean(xn * xn, axis=-1, keepdims=True) + EPS) * g_ref[...]
    if final:
        out_refs[0][...] = hn
    else:
        out_refs[0][...] = xn
        out_refs[1][...] = hn.astype(out_refs[1].dtype)


def _finish(x, oa, obs, lses, zd, zg, mkv, wa, wb, wm, wo, g_next, final):
    b, s, d = x.shape
    t = FIN_T
    row = lambda width: pl.BlockSpec((None, t, width), lambda bb, i: (bb, i, 0))
    full = lambda arr: pl.BlockSpec(arr.shape, lambda bb, i: (0,) * arr.ndim)
    heads = pl.BlockSpec((None, DIL_HEADS, t, HEAD_DIM), lambda bb, i: (bb, 0, i, 0))
    in_specs = [row(d), row(d), heads, heads, heads,
                row(LANES), row(LANES), row(LANES),
                pl.BlockSpec((None, t, d), lambda bb, i: (bb, i, OFF_DG // D_MODEL)),
                row(zg.shape[-1]),
                pl.BlockSpec((None, N_MEM, 2 * MEM_WIDTH), lambda bb, i: (bb, 0, 0)),
                full(wa), full(wb), full(wm), full(wo),
                pl.BlockSpec((1, d), lambda bb, i: (0, 0))]
    if final:
        out_specs = [row(d)]
        out_shape = [jax.ShapeDtypeStruct((b, s, d), jnp.float32)]
    else:
        out_specs = [row(d), row(d)]
        out_shape = [jax.ShapeDtypeStruct((b, s, d), jnp.float32),
                     jax.ShapeDtypeStruct((b, s, d), jnp.bfloat16)]
    return pl.pallas_call(
        functools.partial(_finish_kernel, final=final),
        grid=(b, s // t),
        in_specs=in_specs,
        out_specs=out_specs,
        out_shape=out_shape,
        compiler_params=_params(2),
        name="finish",
    )(x, oa, *obs, *lses, zd, zg, mkv, wa, wb, wm, wo, g_next.reshape(1, d))


def kernel(x, mem, g_norm, w_in, diff_lambda, w_mem_kv, g_mem, w_br_diff, w_br_dil, w_br_mem, w_out,
           rel_bias, g_final):
    b, s, d = x.shape
    depth = w_in.shape[0]
    bf16 = jnp.bfloat16
    m_rows = b * s

    diff_tiles = _diff_bias_tiles(rel_bias, s)
    dil_tiles = [_dil_bias_tiles(rel_bias, g, s) for g in range(len(DIL_GROUPS))]
    mkv_all = _mem_kv(mem.reshape(b * N_MEM, d), g_mem, w_mem_kv)

    col_scale = np.ones((1, N_IN), np.float32)
    col_scale[0, OFF_DQ:OFF_DK] = DIFF_QK_DIM ** -0.5 * LOG2E
    col_scale[0, OFF_LQ:OFF_LK] = HEAD_DIM ** -0.5 * LOG2E
    col_scale[0, OFF_MQ:OFF_MG] = HEAD_DIM ** -0.5 * LOG2E
    col_scale[0, OFF_DG:OFF_LQ] = 0.5
    col_scale[0, OFF_LG:OFF_MQ] = 0.5
    col_scale[0, OFF_MG:] = 0.5
    col_scale = jnp.asarray(col_scale)

    h = _rmsnorm(x.reshape(m_rows, d), g_norm[0], bf16)
    out = None
    for l in range(depth):
        zd = _project(h, w_in, col_scale, l, lambda j, t: 4 * j + t, 4, OFF_LQ // (4 * W_BLOCK),
                      bf16).reshape(b, s, -1)
        zg = _project(h, w_in, col_scale, l, lambda j, t: OFF_LG // W_BLOCK + 3 * j + t, 3,
                      (N_IN - OFF_LG) // (3 * W_BLOCK), bf16).reshape(b, s, -1)
        mkv = mkv_all[l].reshape(b, N_MEM, 2 * MEM_WIDTH)
        lam_init = 0.8 - 0.6 * math.exp(-0.3 * l)
        oa = _diff_attention(zd, diff_tiles, diff_lambda[l], lam_init)
        obs, lses = zip(*[_dil_attention(h, w_in, col_scale, l, g, dil_tiles[g], b)
                          for g in range(len(DIL_GROUPS))])

        final = l == depth - 1
        g_next = g_final if final else g_norm[l + 1]
        res = _finish(x, oa, obs, lses, zd, zg, mkv, w_br_diff[l].astype(bf16), w_br_dil[l].astype(bf16),
                      w_br_mem[l].astype(bf16), (0.5 * w_out[l]).astype(bf16), g_next, final)
        if final:
            out = res[0]
        else:
            x, h3 = res
            h = h3.reshape(m_rows, d)
    return out
```

```python
import functools
import math

import jax
import jax.numpy as jnp
import numpy as np
from jax import lax
from jax.experimental import pallas as pl
from jax.experimental.pallas import tpu as pltpu

D_MODEL = 1024
N_MEM = 256
EPS = 1e-6
NEG_INF = -1e30

DIFF_HEADS = 8
DIFF_QK_DIM = 64
DIFF_V_DIM = 128
DIL_GROUPS = ((128, 1), (512, 4), (2048, 16))
DIL_HEADS = 4
HEAD_DIM = 128
DIL_WIDTH = DIL_HEADS * HEAD_DIM
DIL_HALF = 64
MEM_HEADS = 4
MEM_WIDTH = MEM_HEADS * HEAD_DIM
REL_BUCKETS = 32
REL_MAX_DIST = 1024

OFF_DQ, OFF_DK, OFF_DV, OFF_DG = 0, 1024, 2048, 3072
OFF_LQ, OFF_LK, OFF_LV, OFF_LG = 4096, 5632, 7168, 8704
OFF_MQ, OFF_MG, OFF_MGATE = 9216, 9728, 10240
N_IN = 13312

LOG2E = 1.4426950408889634
LN2 = 0.6931471805599453

LANES = 128
MXU_EDGE = 256
VMEM_LIMIT_BYTES = 56 * 1024 * 1024

DIFF_TQ = 512
DIFF_TK = MXU_EDGE
DIL_TQ = 128
DIL_UNROLL = 4
FIN_T = 512
MM_TM = 2048
MM_SUB = 512
W_BLOCK = 512
STRIDE_STEP = 4


def _params(n_grid_dims):
    return pltpu.CompilerParams(dimension_semantics=("arbitrary",) * n_grid_dims,
                                vmem_limit_bytes=VMEM_LIMIT_BYTES)


def _dot_nt(a, b):
    return lax.dot_general(a, b, (((1,), (1,)), ((), ())), preferred_element_type=jnp.float32)


def _rms_kernel(x_ref, g_ref, o_ref):
    x = x_ref[...]
    ms = jnp.mean(x * x, axis=-1, keepdims=True)
    o_ref[...] = (x * lax.rsqrt(ms + EPS) * g_ref[...]).astype(o_ref.dtype)


def _rmsnorm(x2d, g, out_dtype, tm=512):
    m, d = x2d.shape
    tm = min(tm, m)
    return pl.pallas_call(
        _rms_kernel,
        grid=(m // tm,),
        in_specs=[pl.BlockSpec((tm, d), lambda i: (i, 0)),
                  pl.BlockSpec((1, d), lambda i: (0, 0))],
        out_specs=pl.BlockSpec((tm, d), lambda i: (i, 0)),
        out_shape=jax.ShapeDtypeStruct((m, d), out_dtype),
        compiler_params=_params(1),
        name="rmsnorm",
    )(x2d, g.reshape(1, d))


def _cast_weight_blocks(w_refs, s_refs, wbf_scr):
    for t, (w_ref, s_ref) in enumerate(zip(w_refs, s_refs)):
        wbf_scr[:, t * W_BLOCK:(t + 1) * W_BLOCK] = (w_ref[...] * s_ref[...]).astype(wbf_scr.dtype)


def _proj_kernel(a_ref, *refs, nb):
    w_refs, s_refs, o_ref, wbf_scr = refs[:nb], refs[nb:2 * nb], refs[2 * nb], refs[2 * nb + 1]

    @pl.when(pl.program_id(1) == 0)
    def _():
        _cast_weight_blocks(w_refs, s_refs, wbf_scr)

    sub = min(MM_SUB, a_ref.shape[0])

    def body(s, carry):
        r0 = pl.multiple_of(s * sub, sub)
        acc = jnp.dot(a_ref[pl.ds(r0, sub), :], wbf_scr[...], preferred_element_type=jnp.float32)
        o_ref[pl.ds(r0, sub), :] = acc.astype(o_ref.dtype)
        return carry
    lax.fori_loop(0, a_ref.shape[0] // sub, body, 0, unroll=True)


def _weight_specs(layer, blocks, n_grid):
    def spec(t, weight):
        if n_grid == 2:
            imap = (lambda j, i: (layer, 0, blocks(j, t))) if weight else (lambda j, i: (0, blocks(j, t)))
        else:
            imap = (lambda i: (layer, 0, blocks(0, t))) if weight else (lambda i: (0, blocks(0, t)))
        return pl.BlockSpec((None, D_MODEL, W_BLOCK) if weight else (1, W_BLOCK), imap)
    return spec


def _project(a, w_in, scales, layer, blocks, nb, n_tiles, out_dtype):
    m, k = a.shape
    tm = min(MM_TM, m)
    tn = nb * W_BLOCK
    spec = _weight_specs(layer, blocks, 2)
    return pl.pallas_call(
        functools.partial(_proj_kernel, nb=nb),
        grid=(n_tiles, m // tm),
        in_specs=([pl.BlockSpec((tm, k), lambda j, i: (i, 0))]
                  + [spec(t, True) for t in range(nb)] + [spec(t, False) for t in range(nb)]),
        out_specs=pl.BlockSpec((tm, tn), lambda j, i: (i, j)),
        out_shape=jax.ShapeDtypeStruct((m, n_tiles * tn), out_dtype),
        scratch_shapes=[pltpu.VMEM((k, tn), jnp.bfloat16)],
        compiler_params=_params(2),
        name="in_proj",
    )(a, *([w_in] * nb), *([scales] * nb))


def _project_by_residue(a_ref, wbf_scr, o_ref, scrs, r):
    if r > STRIDE_STEP:
        scrs, tmp = scrs[:-1], scrs[-1]

    rows = a_ref.shape[0]
    n_slabs = wbf_scr.shape[1] // LANES
    sub = min(MM_SUB, rows)
    n_sub = rows // sub
    piece = sub // r

    def matmul_step(t):
        acc = jnp.dot(a_ref[t * sub:(t + 1) * sub, :], wbf_scr[...], preferred_element_type=jnp.float32)
        if r == 1:
            o_ref[0, t * sub:(t + 1) * sub, :] = acc.astype(o_ref.dtype)
        else:
            for k in range(n_slabs):
                scrs[t][k] = acc[:, k * LANES:(k + 1) * LANES]

    def relayout_step(t):
        out_rows = slice(t * piece, (t + 1) * piece)
        for k in range(n_slabs):
            cols = slice(k * LANES, (k + 1) * LANES)
            if r > STRIDE_STEP:
                r2 = r // STRIDE_STEP
                for c1 in range(STRIDE_STEP):
                    tmp[k, c1] = scrs[t][k, pl.ds(c1, sub // STRIDE_STEP, stride=STRIDE_STEP), :]
                for c1 in range(STRIDE_STEP):
                    for c2 in range(r2):
                        o_ref[c1 + STRIDE_STEP * c2, out_rows, cols] = (
                            tmp[k, c1, pl.ds(c2, piece, stride=r2), :].astype(o_ref.dtype))
            else:
                for c in range(r):
                    o_ref[c, out_rows, cols] = scrs[t][k, pl.ds(c, piece, stride=r), :].astype(o_ref.dtype)

    for t in range(n_sub + 1):
        if t < n_sub:
            matmul_step(t)
        if r > 1 and t > 0:
            relayout_step(t - 1)


def _residue_scratch(n, seq, r):
    sub = min(MM_SUB, seq)
    return (([pltpu.VMEM((n // LANES, sub, LANES), jnp.float32) for _ in range(seq // sub)] if r > 1 else [])
            + ([pltpu.VMEM((n // LANES, STRIDE_STEP, sub // STRIDE_STEP, LANES), jnp.float32)]
               if r > STRIDE_STEP else []))


def _cast_cols_kernel(w_ref, s_ref, o_ref):
    o_ref[...] = (w_ref[...] * s_ref[...]).astype(o_ref.dtype)


def _cast_cols(w_in, scales, layer, first_block, n_blocks):
    d = w_in.shape[1]
    return pl.pallas_call(
        _cast_cols_kernel,
        grid=(n_blocks,),
        in_specs=[pl.BlockSpec((None, d, W_BLOCK), lambda j: (layer, 0, first_block + j)),
                  pl.BlockSpec((1, W_BLOCK), lambda j: (0, first_block + j))],
        out_specs=pl.BlockSpec((d, W_BLOCK), lambda j: (0, j)),
        out_shape=jax.ShapeDtypeStruct((d, n_blocks * W_BLOCK), jnp.bfloat16),
        compiler_params=_params(1),
        name="cast_cols",
    )(w_in, scales)


def _mem_kv_kernel(mem_ref, g_ref, w_ref, o_ref):
    x = mem_ref[...]
    ms = jnp.mean(x * x, axis=-1, keepdims=True)
    mem_n = (x * lax.rsqrt(ms + EPS) * g_ref[...]).astype(jnp.bfloat16)
    o_ref[...] = jnp.dot(mem_n, w_ref[...].astype(jnp.bfloat16),
                         preferred_element_type=jnp.float32).astype(o_ref.dtype)


def _mem_kv(mem2d, g_mem, w_mem_kv):
    rows, d = mem2d.shape
    depth, _, n = w_mem_kv.shape
    return pl.pallas_call(
        _mem_kv_kernel,
        grid=(depth,),
        in_specs=[pl.BlockSpec((rows, d), lambda l: (0, 0)),
                  pl.BlockSpec((None, 1, d), lambda l: (l, 0, 0)),
                  pl.BlockSpec((None, d, n), lambda l: (l, 0, 0))],
        out_specs=pl.BlockSpec((None, rows, n), lambda l: (l, 0, 0)),
        out_shape=jax.ShapeDtypeStruct((depth, rows, n), jnp.bfloat16),
        compiler_params=_params(1),
        name="mem_kv",
    )(mem2d, g_mem.reshape(depth, 1, d), w_mem_kv)


def _t5_bucket(rel):
    half = REL_BUCKETS // 2
    max_exact = half // 2
    ret = jnp.where(rel > 0, half, 0)
    n = jnp.abs(rel)
    nf = jnp.maximum(n, 1).astype(jnp.float32)
    large = max_exact + (jnp.log(nf / max_exact) / math.log(REL_MAX_DIST / max_exact)
                         * (half - max_exact)).astype(jnp.int32)
    large = jnp.minimum(large, half - 1)
    return ret + jnp.where(n < max_exact, n, large)


def _toeplitz_kernel(u_ref, o_ref):
    n_tiles, rows, tile_w = o_ref.shape
    x = jnp.broadcast_to(u_ref[...], (rows, u_ref.shape[-1]))
    y = pltpu.roll(x, 0, 1, stride=1, stride_axis=0)
    for d in range(n_tiles):
        o_ref[d] = y[:, d * tile_w:(d + 1) * tile_w]


def _toeplitz_tiles(vals, rows, n_tiles, tile_w):
    groups = vals.shape[0]
    cols = n_tiles * tile_w
    period = pl.next_power_of_2(rows + cols - 1)
    pad = jnp.zeros((groups, period - (rows + cols - 1)), vals.dtype)
    u = jnp.concatenate([vals[:, rows - 1:], pad, vals[:, :rows - 1]], axis=1).reshape(groups, 1, period)
    return pl.pallas_call(
        _toeplitz_kernel,
        grid=(groups,),
        in_specs=[pl.BlockSpec((None, 1, period), lambda g: (g, 0, 0))],
        out_specs=pl.BlockSpec((None, n_tiles, rows, tile_w), lambda g: (g, 0, 0, 0)),
        out_shape=jax.ShapeDtypeStruct((groups, n_tiles, rows, tile_w), vals.dtype),
        compiler_params=_params(1),
        name="toeplitz_tiles",
    )(u)


def _diff_bias_tiles(rel_bias, seq):
    nd = seq // DIFF_TK - 1
    rel = jnp.arange(2 * seq - 1, dtype=jnp.int32) - (seq - 1)
    tvec = jnp.take(rel_bias[:, :DIFF_HEADS], _t5_bucket(rel), axis=0).T * LOG2E
    return _toeplitz_tiles(tvec, DIFF_TK, 2 * nd + 1, DIFF_TK)


def _dil_bias_tiles(rel_bias, g, seq):
    _, r = DIL_GROUPS[g]
    length = seq // r
    width = min(2 * DIL_TQ, length)
    shifts = jnp.array([0, -DIL_HALF, -(width - DIL_TQ)], dtype=jnp.int32)
    delta = shifts[:, None] + jnp.arange(DIL_TQ + width - 1, dtype=jnp.int32)[None, :] - (DIL_TQ - 1)
    c0 = DIFF_HEADS + g * DIL_HEADS
    bias = jnp.take(rel_bias[:, c0:c0 + DIL_HEADS], _t5_bucket(delta * r), axis=0) * LOG2E
    bias = jnp.where((jnp.abs(delta) <= DIL_HALF)[..., None], bias, NEG_INF)
    vals = jnp.transpose(bias, (0, 2, 1)).reshape(3 * DIL_HEADS, DIL_TQ + width - 1)
    return _toeplitz_tiles(vals, DIL_TQ, 1, width).reshape(3, DIL_HEADS, DIL_TQ, width)


def _diff_kernel(q0_ref, q1_ref, k0_ref, k1_ref, v_ref, bias_ref, lam_ref, o_ref,
                 s_scr, mx_scr, vext_scr, acc_scr, *, lam_init, nkc):
    i = pl.program_id(2)
    tq = q0_ref.shape[0]
    nrb = tq // DIFF_TK

    @pl.when(i == 0)
    def _():
        for a in range(2):
            vext_scr[a, :, 0:LANES] = v_ref[:, a * LANES:(a + 1) * LANES]
            vext_scr[a, :, LANES:2 * LANES] = jnp.ones((vext_scr.shape[1], LANES), vext_scr.dtype)

    lp = lam_ref[...]
    lam = (jnp.exp(jnp.sum(lp[0:1] * lp[1:2], axis=-1, keepdims=True))
           - jnp.exp(jnp.sum(lp[2:3] * lp[3:4], axis=-1, keepdims=True)) + lam_init)

    lane = lax.broadcasted_iota(jnp.int32, (tq, LANES), 1)
    q_refs = (q0_ref, q1_ref)
    k_refs = (k0_ref, k1_ref)

    def masked_q(u):
        a, m = divmod(u, 2)
        head_lanes = (lane >= a * DIFF_QK_DIM) & (lane < (a + 1) * DIFF_QK_DIM)
        q = q_refs[m][...]
        return jnp.where(head_lanes, q, jnp.zeros_like(q))

    def qk_chunk(u, qm, j):
        a, m = divmod(u, 2)
        s = _dot_nt(qm, k_refs[m][j * DIFF_TK:(j + 1) * DIFF_TK, :])
        for rb in range(nrb):
            rows = slice(rb * DIFF_TK, (rb + 1) * DIFF_TK)
            sb = s[rows] + bias_ref[a, j - nrb * i - rb + (nkc - 1)]
            s_scr[u, j, rows, :] = sb
            mtile = jnp.maximum(sb[:, :LANES], sb[:, LANES:])
            if j == 0:
                mx_scr[u, rows, :] = mtile
            else:
                mx_scr[u, rows, :] = jnp.maximum(mx_scr[u, rows, :], mtile)

    def row_max(u):
        return jnp.broadcast_to(jnp.max(mx_scr[u], axis=-1, keepdims=True), (tq, DIFF_TK))

    def pv_chunk(u, mb, j, acc):
        e = jnp.exp2(s_scr[u, j] - mb).astype(vext_scr.dtype)
        part = jnp.dot(e, vext_scr[u // 2, j * DIFF_TK:(j + 1) * DIFF_TK, :],
                       preferred_element_type=jnp.float32)
        return part if acc is None else acc + part

    def finish_head(a):
        n0 = acc_scr[2 * a]
        n1 = acc_scr[2 * a + 1]
        out = n0[:, :LANES] / n0[:, LANES:] - lam * (n1[:, :LANES] / n1[:, LANES:])
        ms = jnp.mean(out * out, axis=-1, keepdims=True)
        o_ref[:, a * LANES:(a + 1) * LANES] = (out * lax.rsqrt(ms + EPS) * (1.0 - lam_init)).astype(o_ref.dtype)

    n_units = 4
    mb_prev = None
    for u in range(n_units + 1):
        qm = masked_q(u) if u < n_units else None
        acc = None
        for j in range(nkc):
            if u < n_units:
                qk_chunk(u, qm, j)
            if u > 0:
                acc = pv_chunk(u - 1, mb_prev, j, acc)
        if u > 0:
            acc_scr[u - 1] = acc
            if (u - 1) % 2 == 1:
                finish_head((u - 1) // 2)
        if u < n_units:
            mb_prev = row_max(u)


def _diff_attention(zd, bias_tiles, lam_p, lam_init):
    b, s, _ = zd.shape
    tq = DIFF_TQ
    nkc = s // DIFF_TK
    kern = functools.partial(_diff_kernel, lam_init=lam_init, nkc=nkc)
    kblk = OFF_DK // LANES
    vblk = OFF_DV // (2 * LANES)
    return pl.pallas_call(
        kern,
        grid=(DIFF_HEADS // 2, b, s // tq),
        in_specs=[
            pl.BlockSpec((None, tq, LANES), lambda hp, bb, i: (bb, i, hp)),
            pl.BlockSpec((None, tq, LANES), lambda hp, bb, i: (bb, i, DIFF_HEADS // 2 + hp)),
            pl.BlockSpec((None, s, LANES), lambda hp, bb, i: (bb, 0, kblk + hp)),
            pl.BlockSpec((None, s, LANES), lambda hp, bb, i: (bb, 0, kblk + DIFF_HEADS // 2 + hp)),
            pl.BlockSpec((None, s, 2 * LANES), lambda hp, bb, i: (bb, 0, vblk + hp)),
            pl.BlockSpec((2, 2 * nkc - 1, DIFF_TK, DIFF_TK), lambda hp, bb, i: (hp, 0, 0, 0)),
            pl.BlockSpec((4, DIFF_QK_DIM), lambda hp, bb, i: (0, 0)),
        ],
        out_specs=pl.BlockSpec((None, tq, 2 * LANES), lambda hp, bb, i: (bb, i, hp)),
        out_shape=jax.ShapeDtypeStruct((b, s, DIFF_HEADS * DIFF_V_DIM), jnp.bfloat16),
        scratch_shapes=[
            pltpu.VMEM((4, nkc, tq, DIFF_TK), jnp.float32),
            pltpu.VMEM((4, tq, LANES), jnp.float32),
            pltpu.VMEM((2, s, 2 * LANES), jnp.bfloat16),
            pltpu.VMEM((4, tq, 2 * LANES), jnp.float32),
        ],
        compiler_params=_params(3),
        name="diff_attn",
    )(zd, zd, zd, zd, zd, bias_tiles, lam_p)


def _dil_kernel(a_ref, *refs, nb, r, length, width):
    w_refs, s_refs = refs[:nb], refs[nb:2 * nb]
    bias_ref, o_ref, lse_ref, wbf_scr, z_ref = refs[2 * nb:2 * nb + 5]

    @pl.when(pl.program_id(0) == 0)
    def _():
        _cast_weight_blocks(w_refs, s_refs, wbf_scr)

    _project_by_residue(a_ref, wbf_scr, z_ref, refs[2 * nb + 5:], r)

    nqb = length // DIL_TQ
    lane = lax.broadcasted_iota(jnp.int32, (DIL_TQ, LANES), 1)

    def block_coords(t):
        if nqb == 1:
            return t, 0
        if r == 1:
            return 0, t
        return t // nqb, t % nqb

    def body(tt, carry):
        staged = []
        for i in range(DIL_UNROLL):
            c, qb = block_coords(tt * DIL_UNROLL + i)
            q0 = pl.multiple_of(qb * DIL_TQ, DIL_TQ)
            ws = pl.multiple_of(jnp.clip(q0 - DIL_HALF, 0, length - width), DIL_HALF)
            var = jnp.where(qb == 0, 0, jnp.where(qb == nqb - 1, 2, 1))
            scs = []
            for h in range(DIL_HEADS):
                col = h * HEAD_DIM
                q = z_ref[c, pl.ds(q0, DIL_TQ), col:col + HEAD_DIM]
                kw = z_ref[c, pl.ds(ws, width), col + DIL_WIDTH:col + DIL_WIDTH + HEAD_DIM]
                scs.append(_dot_nt(q, kw) + bias_ref[var, h])
            staged.append((c, q0, ws, scs))
        for c, q0, ws, scs in staged:
            out_rows = pl.ds(q0 * r + c, DIL_TQ, stride=r) if r > 1 else pl.ds(q0, DIL_TQ)
            lse_tile = jnp.zeros((DIL_TQ, LANES), jnp.float32)
            for h in range(DIL_HEADS):
                col = 2 * DIL_WIDTH + h * HEAD_DIM
                vw = z_ref[c, pl.ds(ws, width), col:col + HEAD_DIM]
                sc = scs[h]
                mrow = jnp.max(sc, axis=-1, keepdims=True)
                e = jnp.exp2(sc - mrow)
                den = jnp.sum(e, axis=-1, keepdims=True)
                o_ref[h, out_rows, :] = jnp.dot(e.astype(vw.dtype), vw,
                                                preferred_element_type=jnp.float32) / den
                lse_tile = jnp.where(lane == h, mrow + jnp.log2(den), lse_tile)
            lse_ref[out_rows, :] = lse_tile
        return carry
    lax.fori_loop(0, r * nqb // DIL_UNROLL, body, 0)


def _dil_attention(h, w_in, scales, layer, g, bias_tiles, batch):
    m, k = h.shape
    s = m // batch
    _, r = DIL_GROUPS[g]
    length = s // r
    nb = 3
    n = nb * W_BLOCK
    width = bias_tiles.shape[-1]
    spec = _weight_specs(layer, lambda j, t: (OFF_LQ, OFF_LK, OFF_LV)[t] // W_BLOCK + g, 1)
    kern = functools.partial(_dil_kernel, nb=nb, r=r, length=length, width=width)
    return pl.pallas_call(
        kern,
        grid=(batch,),
        in_specs=([pl.BlockSpec((s, k), lambda bb: (bb, 0))]
                  + [spec(t, True) for t in range(nb)] + [spec(t, False) for t in range(nb)]
                  + [pl.BlockSpec(bias_tiles.shape, lambda bb: (0, 0, 0, 0))]),
        out_specs=[pl.BlockSpec((None, DIL_HEADS, s, HEAD_DIM), lambda bb: (bb, 0, 0, 0)),
                   pl.BlockSpec((None, s, LANES), lambda bb: (bb, 0, 0))],
        out_shape=[jax.ShapeDtypeStruct((batch, DIL_HEADS, s, HEAD_DIM), jnp.float32),
                   jax.ShapeDtypeStruct((batch, s, LANES), jnp.float32)],
        scratch_shapes=([pltpu.VMEM((k, n), jnp.bfloat16), pltpu.VMEM((r, length, n), jnp.bfloat16)]
                        + _residue_scratch(n, s, r)),
        compiler_params=_params(1),
        name=f"dil_attn_g{g}",
    )(h, *([w_in] * nb), *([scales] * nb), bias_tiles)


def _finish_kernel(x_ref, oa_ref, ob0_ref, ob1_ref, ob2_ref, l0_ref, l1_ref, l2_ref, dg_ref, h_ref, wg_ref,
                   mkv_ref, wa_ref, wb_ref, wm_ref, wo_ref, g_ref, *refs, final):
    out_refs, zg_ref = refs[:-1], refs[-1]
    bf16 = jnp.bfloat16
    f32 = jnp.float32

    pending = list(range(wg_ref.shape[1] // W_BLOCK))

    def project(n):
        for _ in range(n):
            cb = pending.pop(0)
            cols = slice(cb * W_BLOCK, (cb + 1) * W_BLOCK)
            zg_ref[:, cols] = jnp.dot(h_ref[...], wg_ref[:, cols], preferred_element_type=f32).astype(zg_ref.dtype)

    def half_silu(zh):
        zh = zh.astype(f32)
        return zh + zh * jnp.tanh(zh)

    c_lg, c_mq, c_mg, c_gate = 0, DIL_WIDTH, DIL_WIDTH + MEM_WIDTH, DIL_WIDTH + 2 * MEM_WIDTH

    project(3)

    ya = jnp.dot((oa_ref[...].astype(f32) * half_silu(dg_ref[...])).astype(bf16), wa_ref[...],
                 preferred_element_type=jnp.float32)
    project(2)

    l0, l1, l2 = l0_ref[...], l1_ref[...], l2_ref[...]
    lmax = jnp.maximum(jnp.maximum(l0, l1), l2)
    w0, w1, w2 = jnp.exp2(l0 - lmax), jnp.exp2(l1 - lmax), jnp.exp2(l2 - lmax)
    inv = 1.0 / (w0 + w1 + w2)
    w0, w1, w2 = w0 * inv, w1 * inv, w2 * inv
    ob_refs = (ob0_ref, ob1_ref, ob2_ref)
    parts = []
    for h in range(DIL_HEADS):
        acc = None
        for wg, ob_ref in zip((w0, w1, w2), ob_refs):
            term = wg[:, h:h + 1] * ob_ref[h]
            acc = term if acc is None else acc + term
        parts.append(acc)
    ob = jnp.concatenate(parts, axis=-1)
    yb = jnp.dot((ob * half_silu(zg_ref[:, c_lg:c_lg + DIL_WIDTH])).astype(bf16), wb_ref[...],
                 preferred_element_type=jnp.float32)
    project(2)

    scs = [_dot_nt(zg_ref[:, c_mq + h * HEAD_DIM:c_mq + (h + 1) * HEAD_DIM],
                   mkv_ref[:, h * HEAD_DIM:(h + 1) * HEAD_DIM]) for h in range(MEM_HEADS)]
    project(1)
    parts = []
    for h in range(MEM_HEADS):
        sc = scs[h]
        e = jnp.exp2(sc - jnp.max(sc, axis=-1, keepdims=True))
        den = jnp.sum(e, axis=-1, keepdims=True)
        vcols = slice(MEM_WIDTH + h * HEAD_DIM, MEM_WIDTH + (h + 1) * HEAD_DIM)
        parts.append(jnp.dot(e.astype(bf16), mkv_ref[:, vcols], preferred_element_type=jnp.float32) / den)
    om = jnp.concatenate(parts, axis=-1)
    project(1)
    ym = jnp.dot((om * half_silu(zg_ref[:, c_mg:c_mg + MEM_WIDTH])).astype(bf16), wm_ref[...],
                 preferred_element_type=jnp.float32)
    project(len(pending))

    merged = (ya + yb + ym
              + jnp.tanh(zg_ref[:, c_gate:c_gate + D_MODEL].astype(f32)) * ya
              + jnp.tanh(zg_ref[:, c_gate + D_MODEL:c_gate + 2 * D_MODEL].astype(f32)) * yb
              + jnp.tanh(zg_ref[:, c_gate + 2 * D_MODEL:c_gate + 3 * D_MODEL].astype(f32)) * ym)
    xn = x_ref[...] + jnp.dot(merged.astype(bf16), wo_ref[...], preferred_element_type=jnp.float32)
    hn = xn * lax.rsqrt(jnp.mean(xn * xn, axis=-1, keepdims=True) + EPS) * g_ref[...]
    if final:
        out_refs[0][...] = hn
    else:
        out_refs[0][...] = xn
        out_refs[1][...] = hn.astype(out_refs[1].dtype)


def _finish(x, oa, obs, lses, zd, h, wg, mkv, wa, wb, wm, wo, g_next, final):
    b, s, d = x.shape
    t = FIN_T
    row = lambda width: pl.BlockSpec((None, t, width), lambda bb, i: (bb, i, 0))
    full = lambda arr: pl.BlockSpec(arr.shape, lambda bb, i: (0,) * arr.ndim)
    heads = pl.BlockSpec((None, DIL_HEADS, t, HEAD_DIM), lambda bb, i: (bb, 0, i, 0))
    in_specs = [row(d), row(d), heads, heads, heads,
                row(LANES), row(LANES), row(LANES),
                pl.BlockSpec((None, t, d), lambda bb, i: (bb, i, OFF_DG // D_MODEL)),
                row(d), full(wg),
                pl.BlockSpec((None, N_MEM, 2 * MEM_WIDTH), lambda bb, i: (bb, 0, 0)),
                full(wa), full(wb), full(wm), full(wo),
                pl.BlockSpec((1, d), lambda bb, i: (0, 0))]
    if final:
        out_specs = [row(d)]
        out_shape = [jax.ShapeDtypeStruct((b, s, d), jnp.float32)]
    else:
        out_specs = [row(d), row(d)]
        out_shape = [jax.ShapeDtypeStruct((b, s, d), jnp.float32),
                     jax.ShapeDtypeStruct((b, s, d), jnp.bfloat16)]
    return pl.pallas_call(
        functools.partial(_finish_kernel, final=final),
        grid=(b, s // t),
        in_specs=in_specs,
        out_specs=out_specs,
        out_shape=out_shape,
        scratch_shapes=[pltpu.VMEM((t, wg.shape[1]), jnp.bfloat16)],
        compiler_params=_params(2),
        name="finish",
    )(x, oa, *obs, *lses, zd, h, wg, mkv, wa, wb, wm, wo, g_next.reshape(1, d))


def kernel(x, mem, g_norm, w_in, diff_lambda, w_mem_kv, g_mem, w_br_diff, w_br_dil, w_br_mem, w_out,
           rel_bias, g_final):
    b, s, d = x.shape
    depth = w_in.shape[0]
    bf16 = jnp.bfloat16
    m_rows = b * s

    diff_tiles = _diff_bias_tiles(rel_bias, s)
    dil_tiles = [_dil_bias_tiles(rel_bias, g, s) for g in range(len(DIL_GROUPS))]
    mkv_all = _mem_kv(mem.reshape(b * N_MEM, d), g_mem, w_mem_kv)

    col_scale = np.ones((1, N_IN), np.float32)
    col_scale[0, OFF_DQ:OFF_DK] = DIFF_QK_DIM ** -0.5 * LOG2E
    col_scale[0, OFF_LQ:OFF_LK] = HEAD_DIM ** -0.5 * LOG2E
    col_scale[0, OFF_MQ:OFF_MG] = HEAD_DIM ** -0.5 * LOG2E
    col_scale[0, OFF_DG:OFF_LQ] = 0.5
    col_scale[0, OFF_LG:OFF_MQ] = 0.5
    col_scale[0, OFF_MG:] = 0.5
    col_scale = jnp.asarray(col_scale)

    h = _rmsnorm(x.reshape(m_rows, d), g_norm[0], bf16)
    out = None
    for l in range(depth):
        zd = _project(h, w_in, col_scale, l, lambda j, t: 4 * j + t, 4, OFF_LQ // (4 * W_BLOCK),
                      bf16).reshape(b, s, -1)
        wg = _cast_cols(w_in, col_scale, l, OFF_LG // W_BLOCK, (N_IN - OFF_LG) // W_BLOCK)
        mkv = mkv_all[l].reshape(b, N_MEM, 2 * MEM_WIDTH)
        lam_init = 0.8 - 0.6 * math.exp(-0.3 * l)
        oa = _diff_attention(zd, diff_tiles, diff_lambda[l], lam_init)
        obs, lses = zip(*[_dil_attention(h, w_in, col_scale, l, g, dil_tiles[g], b)
                          for g in range(len(DIL_GROUPS))])

        final = l == depth - 1
        g_next = g_final if final else g_norm[l + 1]
        res = _finish(x, oa, obs, lses, zd, h.reshape(b, s, d), wg, mkv,
                      w_br_diff[l].astype(bf16), w_br_dil[l].astype(bf16),
                      w_br_mem[l].astype(bf16), (0.5 * w_out[l]).astype(bf16), g_next, final)
        if final:
            out = res[0]
        else:
            x, h3 = res
            h = h3.reshape(m_rows, d)
    return out
```

```python
import functools
import math

import jax
import jax.numpy as jnp
import numpy as np
from jax import lax
from jax.experimental import pallas as pl
from jax.experimental.pallas import tpu as pltpu

D_MODEL = 1024
N_MEM = 256
EPS = 1e-6
NEG_INF = -1e30

DIFF_HEADS = 8
DIFF_QK_DIM = 64
DIFF_V_DIM = 128
DIL_GROUPS = ((128, 1), (512, 4), (2048, 16))
DIL_HEADS = 4
HEAD_DIM = 128
DIL_WIDTH = DIL_HEADS * HEAD_DIM
DIL_HALF = 64
MEM_HEADS = 4
MEM_WIDTH = MEM_HEADS * HEAD_DIM
REL_BUCKETS = 32
REL_MAX_DIST = 1024

OFF_DQ, OFF_DK, OFF_DV, OFF_DG = 0, 1024, 2048, 3072
OFF_LQ, OFF_LK, OFF_LV, OFF_LG = 4096, 5632, 7168, 8704
OFF_MQ, OFF_MG, OFF_MGATE = 9216, 9728, 10240
N_IN = 13312

LOG2E = 1.4426950408889634
LN2 = 0.6931471805599453

LANES = 128
MXU_EDGE = 256
VMEM_LIMIT_BYTES = 56 * 1024 * 1024

DIFF_TQ = 512
DIFF_TK = MXU_EDGE
DIFF_SKEW = 2
DIFF_BOUND_SLACK = 1.0 + 2.0 ** -8
DIFF_MIN_LOG2_SUM = -80.0
DIL_TQ = 128
DIL_UNROLL = 4
FIN_T = 512
MM_TM = 2048
MM_SUB = 512
W_BLOCK = 512
STRIDE_STEP = 4


def _params(n_grid_dims):
    return pltpu.CompilerParams(dimension_semantics=("arbitrary",) * n_grid_dims,
                                vmem_limit_bytes=VMEM_LIMIT_BYTES)


def _dot_nt(a, b):
    return lax.dot_general(a, b, (((1,), (1,)), ((), ())), preferred_element_type=jnp.float32)


def _rms_kernel(x_ref, g_ref, o_ref):
    x = x_ref[...]
    ms = jnp.mean(x * x, axis=-1, keepdims=True)
    o_ref[...] = (x * lax.rsqrt(ms + EPS) * g_ref[...]).astype(o_ref.dtype)


def _rmsnorm(x2d, g, out_dtype, tm=512):
    m, d = x2d.shape
    tm = min(tm, m)
    return pl.pallas_call(
        _rms_kernel,
        grid=(m // tm,),
        in_specs=[pl.BlockSpec((tm, d), lambda i: (i, 0)),
                  pl.BlockSpec((1, d), lambda i: (0, 0))],
        out_specs=pl.BlockSpec((tm, d), lambda i: (i, 0)),
        out_shape=jax.ShapeDtypeStruct((m, d), out_dtype),
        compiler_params=_params(1),
        name="rmsnorm",
    )(x2d, g.reshape(1, d))


def _cast_weight_blocks(w_refs, s_refs, wbf_scr):
    for t, (w_ref, s_ref) in enumerate(zip(w_refs, s_refs)):
        wbf_scr[:, t * W_BLOCK:(t + 1) * W_BLOCK] = (w_ref[...] * s_ref[...]).astype(wbf_scr.dtype)


def _proj_kernel(a_ref, *refs, nb):
    w_refs, s_refs, o_ref, wbf_scr = refs[:nb], refs[nb:2 * nb], refs[2 * nb], refs[2 * nb + 1]

    @pl.when(pl.program_id(1) == 0)
    def _():
        _cast_weight_blocks(w_refs, s_refs, wbf_scr)

    sub = min(MM_SUB, a_ref.shape[0])

    def body(s, carry):
        r0 = pl.multiple_of(s * sub, sub)
        acc = jnp.dot(a_ref[pl.ds(r0, sub), :], wbf_scr[...], preferred_element_type=jnp.float32)
        o_ref[pl.ds(r0, sub), :] = acc.astype(o_ref.dtype)
        return carry
    lax.fori_loop(0, a_ref.shape[0] // sub, body, 0, unroll=True)


def _weight_specs(layer, blocks, n_grid):
    def spec(t, weight):
        if n_grid == 2:
            imap = (lambda j, i: (layer, 0, blocks(j, t))) if weight else (lambda j, i: (0, blocks(j, t)))
        else:
            imap = (lambda i: (layer, 0, blocks(0, t))) if weight else (lambda i: (0, blocks(0, t)))
        return pl.BlockSpec((None, D_MODEL, W_BLOCK) if weight else (1, W_BLOCK), imap)
    return spec


def _project(a, w_in, scales, layer, blocks, nb, n_tiles, out_dtype):
    m, k = a.shape
    tm = min(MM_TM, m)
    tn = nb * W_BLOCK
    spec = _weight_specs(layer, blocks, 2)
    return pl.pallas_call(
        functools.partial(_proj_kernel, nb=nb),
        grid=(n_tiles, m // tm),
        in_specs=([pl.BlockSpec((tm, k), lambda j, i: (i, 0))]
                  + [spec(t, True) for t in range(nb)] + [spec(t, False) for t in range(nb)]),
        out_specs=pl.BlockSpec((tm, tn), lambda j, i: (i, j)),
        out_shape=jax.ShapeDtypeStruct((m, n_tiles * tn), out_dtype),
        scratch_shapes=[pltpu.VMEM((k, tn), jnp.bfloat16)],
        compiler_params=_params(2),
        name="in_proj",
    )(a, *([w_in] * nb), *([scales] * nb))


def _project_by_residue(a_ref, wbf_scr, o_ref, scrs, r):
    if r > STRIDE_STEP:
        scrs, tmp = scrs[:-1], scrs[-1]

    rows = a_ref.shape[0]
    n_slabs = wbf_scr.shape[1] // LANES
    sub = min(MM_SUB, rows)
    n_sub = rows // sub
    piece = sub // r

    def matmul_step(t):
        acc = jnp.dot(a_ref[t * sub:(t + 1) * sub, :], wbf_scr[...], preferred_element_type=jnp.float32)
        if r == 1:
            o_ref[0, t * sub:(t + 1) * sub, :] = acc.astype(o_ref.dtype)
        else:
            for k in range(n_slabs):
                scrs[t][k] = acc[:, k * LANES:(k + 1) * LANES]

    def relayout_step(t):
        out_rows = slice(t * piece, (t + 1) * piece)
        for k in range(n_slabs):
            cols = slice(k * LANES, (k + 1) * LANES)
            if r > STRIDE_STEP:
                r2 = r // STRIDE_STEP
                for c1 in range(STRIDE_STEP):
                    tmp[k, c1] = scrs[t][k, pl.ds(c1, sub // STRIDE_STEP, stride=STRIDE_STEP), :]
                for c1 in range(STRIDE_STEP):
                    for c2 in range(r2):
                        o_ref[c1 + STRIDE_STEP * c2, out_rows, cols] = (
                            tmp[k, c1, pl.ds(c2, piece, stride=r2), :].astype(o_ref.dtype))
            else:
                for c in range(r):
                    o_ref[c, out_rows, cols] = scrs[t][k, pl.ds(c, piece, stride=r), :].astype(o_ref.dtype)

    for t in range(n_sub + 1):
        if t < n_sub:
            matmul_step(t)
        if r > 1 and t > 0:
            relayout_step(t - 1)


def _residue_scratch(n, seq, r):
    sub = min(MM_SUB, seq)
    return (([pltpu.VMEM((n // LANES, sub, LANES), jnp.float32) for _ in range(seq // sub)] if r > 1 else [])
            + ([pltpu.VMEM((n // LANES, STRIDE_STEP, sub // STRIDE_STEP, LANES), jnp.float32)]
               if r > STRIDE_STEP else []))


def _cast_cols_kernel(w_ref, s_ref, o_ref):
    o_ref[...] = (w_ref[...] * s_ref[...]).astype(o_ref.dtype)


def _cast_cols(w_in, scales, layer, first_block, n_blocks):
    d = w_in.shape[1]
    return pl.pallas_call(
        _cast_cols_kernel,
        grid=(n_blocks,),
        in_specs=[pl.BlockSpec((None, d, W_BLOCK), lambda j: (layer, 0, first_block + j)),
                  pl.BlockSpec((1, W_BLOCK), lambda j: (0, first_block + j))],
        out_specs=pl.BlockSpec((d, W_BLOCK), lambda j: (0, j)),
        out_shape=jax.ShapeDtypeStruct((d, n_blocks * W_BLOCK), jnp.bfloat16),
        compiler_params=_params(1),
        name="cast_cols",
    )(w_in, scales)


def _mem_kv_kernel(mem_ref, g_ref, w_ref, o_ref):
    x = mem_ref[...]
    ms = jnp.mean(x * x, axis=-1, keepdims=True)
    mem_n = (x * lax.rsqrt(ms + EPS) * g_ref[...]).astype(jnp.bfloat16)
    o_ref[...] = jnp.dot(mem_n, w_ref[...].astype(jnp.bfloat16),
                         preferred_element_type=jnp.float32).astype(o_ref.dtype)


def _mem_kv(mem2d, g_mem, w_mem_kv):
    rows, d = mem2d.shape
    depth, _, n = w_mem_kv.shape
    return pl.pallas_call(
        _mem_kv_kernel,
        grid=(depth,),
        in_specs=[pl.BlockSpec((rows, d), lambda l: (0, 0)),
                  pl.BlockSpec((None, 1, d), lambda l: (l, 0, 0)),
                  pl.BlockSpec((None, d, n), lambda l: (l, 0, 0))],
        out_specs=pl.BlockSpec((None, rows, n), lambda l: (l, 0, 0)),
        out_shape=jax.ShapeDtypeStruct((depth, rows, n), jnp.bfloat16),
        compiler_params=_params(1),
        name="mem_kv",
    )(mem2d, g_mem.reshape(depth, 1, d), w_mem_kv)


def _t5_bucket(rel):
    half = REL_BUCKETS // 2
    max_exact = half // 2
    ret = jnp.where(rel > 0, half, 0)
    n = jnp.abs(rel)
    nf = jnp.maximum(n, 1).astype(jnp.float32)
    large = max_exact + (jnp.log(nf / max_exact) / math.log(REL_MAX_DIST / max_exact)
                         * (half - max_exact)).astype(jnp.int32)
    large = jnp.minimum(large, half - 1)
    return ret + jnp.where(n < max_exact, n, large)


def _toeplitz_kernel(u_ref, o_ref):
    n_tiles, rows, tile_w = o_ref.shape
    x = jnp.broadcast_to(u_ref[...], (rows, u_ref.shape[-1]))
    y = pltpu.roll(x, 0, 1, stride=1, stride_axis=0)
    for d in range(n_tiles):
        o_ref[d] = y[:, d * tile_w:(d + 1) * tile_w]


def _toeplitz_tiles(vals, rows, n_tiles, tile_w):
    groups = vals.shape[0]
    cols = n_tiles * tile_w
    period = pl.next_power_of_2(rows + cols - 1)
    pad = jnp.zeros((groups, period - (rows + cols - 1)), vals.dtype)
    u = jnp.concatenate([vals[:, rows - 1:], pad, vals[:, :rows - 1]], axis=1).reshape(groups, 1, period)
    return pl.pallas_call(
        _toeplitz_kernel,
        grid=(groups,),
        in_specs=[pl.BlockSpec((None, 1, period), lambda g: (g, 0, 0))],
        out_specs=pl.BlockSpec((None, n_tiles, rows, tile_w), lambda g: (g, 0, 0, 0)),
        out_shape=jax.ShapeDtypeStruct((groups, n_tiles, rows, tile_w), vals.dtype),
        compiler_params=_params(1),
        name="toeplitz_tiles",
    )(u)


def _diff_bias_tiles(rel_bias, seq):
    nd = seq // DIFF_TK - 1
    rel = jnp.arange(2 * seq - 1, dtype=jnp.int32) - (seq - 1)
    tvec = jnp.take(rel_bias[:, :DIFF_HEADS], _t5_bucket(rel), axis=0).T * LOG2E
    bias_max = jnp.broadcast_to(jnp.max(tvec, axis=1)[:, None, None], (DIFF_HEADS, 8, LANES))
    return _toeplitz_tiles(tvec, DIFF_TK, 2 * nd + 1, DIFF_TK), bias_max


def _dil_bias_tiles(rel_bias, g, seq):
    _, r = DIL_GROUPS[g]
    length = seq // r
    width = min(2 * DIL_TQ, length)
    shifts = jnp.array([0, -DIL_HALF, -(width - DIL_TQ)], dtype=jnp.int32)
    delta = shifts[:, None] + jnp.arange(DIL_TQ + width - 1, dtype=jnp.int32)[None, :] - (DIL_TQ - 1)
    c0 = DIFF_HEADS + g * DIL_HEADS
    bias = jnp.take(rel_bias[:, c0:c0 + DIL_HEADS], _t5_bucket(delta * r), axis=0) * LOG2E
    bias = jnp.where((jnp.abs(delta) <= DIL_HALF)[..., None], bias, NEG_INF)
    vals = jnp.transpose(bias, (0, 2, 1)).reshape(3 * DIL_HEADS, DIL_TQ + width - 1)
    return _toeplitz_tiles(vals, DIL_TQ, 1, width).reshape(3, DIL_HEADS, DIL_TQ, width)


def _diff_kernel(q0_ref, q1_ref, k0_ref, k1_ref, v_ref, bias_ref, bmax_ref, lam_ref, o_ref,
                 s_scr, mx_scr, vext_scr, acc_scr, kmax_scr, *, lam_init, nkc):
    i = pl.program_id(2)
    tq = q0_ref.shape[0]
    nrb = tq // DIFF_TK

    @pl.when(i == 0)
    def _():
        for a in range(2):
            vext_scr[a, :, 0:LANES] = v_ref[:, a * LANES:(a + 1) * LANES]
            vext_scr[a, :, LANES:2 * LANES] = jnp.ones((vext_scr.shape[1], LANES), vext_scr.dtype)
        hlane = lax.broadcasted_iota(jnp.int32, (1, LANES), 1)
        for m, k_ref in enumerate((k0_ref, k1_ref)):
            col_max = jnp.max(jnp.square(k_ref[...].astype(jnp.float32)), axis=0, keepdims=True)
            for a in range(2):
                in_head = (hlane >= a * DIFF_QK_DIM) & (hlane < (a + 1) * DIFF_QK_DIM)
                kmax = jnp.sqrt(jnp.sum(jnp.where(in_head, col_max, 0.0), axis=-1, keepdims=True))
                kmax_scr[2 * a + m] = jnp.broadcast_to(kmax, kmax_scr.shape[1:])

    lp = lam_ref[...]
    lam = (jnp.exp(jnp.sum(lp[0:1] * lp[1:2], axis=-1, keepdims=True))
           - jnp.exp(jnp.sum(lp[2:3] * lp[3:4], axis=-1, keepdims=True)) + lam_init)

    lane = lax.broadcasted_iota(jnp.int32, (tq, LANES), 1)
    q_refs = (q0_ref, q1_ref)
    k_refs = (k0_ref, k1_ref)

    def masked_q(u):
        a, m = divmod(u, 2)
        head_lanes = (lane >= a * DIFF_QK_DIM) & (lane < (a + 1) * DIFF_QK_DIM)
        q = q_refs[m][...]
        return jnp.where(head_lanes, q, jnp.zeros_like(q))

    def qk_chunk(u, qm, j):
        a, m = divmod(u, 2)
        s = _dot_nt(qm, k_refs[m][j * DIFF_TK:(j + 1) * DIFF_TK, :])
        for rb in range(nrb):
            rows = slice(rb * DIFF_TK, (rb + 1) * DIFF_TK)
            sb = s[rows] + bias_ref[a, j - nrb * i - rb + (nkc - 1)]
            s_scr[u, j, rows, :] = sb
            mtile = jnp.maximum(sb[:, :LANES], sb[:, LANES:])
            if j == 0:
                mx_scr[u, rows, :] = mtile
            else:
                mx_scr[u, rows, :] = jnp.maximum(mx_scr[u, rows, :], mtile)

    def row_max(u):
        return jnp.broadcast_to(jnp.max(mx_scr[u], axis=-1, keepdims=True), (tq, DIFF_TK))

    def pv_chunk(u, mb, j, acc):
        e = jnp.exp2(s_scr[u, j] - mb).astype(vext_scr.dtype)
        part = jnp.dot(e, vext_scr[u // 2, j * DIFF_TK:(j + 1) * DIFF_TK, :],
                       preferred_element_type=jnp.float32)
        return part if acc is None else acc + part

    def finish_head(a):
        n0 = acc_scr[2 * a]
        n1 = acc_scr[2 * a + 1]
        out = n0[:, :LANES] / n0[:, LANES:] - lam * (n1[:, :LANES] / n1[:, LANES:])
        ms = jnp.mean(out * out, axis=-1, keepdims=True)
        o_ref[:, a * LANES:(a + 1) * LANES] = (out * lax.rsqrt(ms + EPS) * (1.0 - lam_init)).astype(o_ref.dtype)

    n_units = 4

    def row_bound(u):
        a, m = divmod(u, 2)
        head_lanes = (lane >= a * DIFF_QK_DIM) & (lane < (a + 1) * DIFF_QK_DIM)
        qsq = jnp.square(q_refs[m][...].astype(jnp.float32))
        qn = jnp.sqrt(jnp.sum(jnp.where(head_lanes, qsq, 0.0), axis=-1, keepdims=True))
        ub = qn * (kmax_scr[u][0:1, 0:1] * DIFF_BOUND_SLACK) + bmax_ref[a][0:1, 0:1]
        return jnp.broadcast_to(ub, (tq, DIFF_TK))

    items = [(u, j) for u in range(n_units) for j in range(nkc)]
    qms, ubs, logits, accs = {}, {}, {}, {}
    min_sum = None
    for n in range(len(items) + DIFF_SKEW):
        if n < len(items):
            u, j = items[n]
            if j == 0:
                qms[u] = masked_q(u)
                ubs[u] = row_bound(u)
            a, m = divmod(u, 2)
            logits[n] = _dot_nt(qms[u], k_refs[m][j * DIFF_TK:(j + 1) * DIFF_TK, :])
        if n >= DIFF_SKEW:
            u, j = items[n - DIFF_SKEW]
            a = u // 2
            s = logits.pop(n - DIFF_SKEW)
            e = jnp.concatenate(
                [jnp.exp2(s[rb * DIFF_TK:(rb + 1) * DIFF_TK] + bias_ref[a, j - nrb * i - rb + (nkc - 1)]
                          - ubs[u][rb * DIFF_TK:(rb + 1) * DIFF_TK]) for rb in range(nrb)],
                axis=0).astype(vext_scr.dtype)
            part = jnp.dot(e, vext_scr[a, j * DIFF_TK:(j + 1) * DIFF_TK, :], preferred_element_type=jnp.float32)
            accs[u] = part if j == 0 else accs[u] + part
            if j == nkc - 1:
                acc = accs.pop(u)
                acc_scr[u] = acc
                row_sum_min = jnp.min(acc[:, LANES:])
                min_sum = row_sum_min if min_sum is None else jnp.minimum(min_sum, row_sum_min)
                if u % 2 == 1:
                    finish_head(u // 2)

    @pl.when(min_sum < 2.0 ** DIFF_MIN_LOG2_SUM)
    def _():
        mb_prev = None
        for u in range(n_units + 1):
            qm = masked_q(u) if u < n_units else None
            acc = None
            for j in range(nkc):
                if u < n_units:
                    qk_chunk(u, qm, j)
                if u > 0:
                    acc = pv_chunk(u - 1, mb_prev, j, acc)
            if u > 0:
                acc_scr[u - 1] = acc
                if (u - 1) % 2 == 1:
                    finish_head((u - 1) // 2)
            if u < n_units:
                mb_prev = row_max(u)


def _diff_attention(zd, bias_tiles, bias_max, lam_p, lam_init):
    b, s, _ = zd.shape
    tq = DIFF_TQ
    nkc = s // DIFF_TK
    kern = functools.partial(_diff_kernel, lam_init=lam_init, nkc=nkc)
    kblk = OFF_DK // LANES
    vblk = OFF_DV // (2 * LANES)
    return pl.pallas_call(
        kern,
        grid=(DIFF_HEADS // 2, b, s // tq),
        in_specs=[
            pl.BlockSpec((None, tq, LANES), lambda hp, bb, i: (bb, i, hp)),
            pl.BlockSpec((None, tq, LANES), lambda hp, bb, i: (bb, i, DIFF_HEADS // 2 + hp)),
            pl.BlockSpec((None, s, LANES), lambda hp, bb, i: (bb, 0, kblk + hp)),
            pl.BlockSpec((None, s, LANES), lambda hp, bb, i: (bb, 0, kblk + DIFF_HEADS // 2 + hp)),
            pl.BlockSpec((None, s, 2 * LANES), lambda hp, bb, i: (bb, 0, vblk + hp)),
            pl.BlockSpec((2, 2 * nkc - 1, DIFF_TK, DIFF_TK), lambda hp, bb, i: (hp, 0, 0, 0)),
            pl.BlockSpec((2, 8, LANES), lambda hp, bb, i: (hp, 0, 0)),
            pl.BlockSpec((4, DIFF_QK_DIM), lambda hp, bb, i: (0, 0)),
        ],
        out_specs=pl.BlockSpec((None, tq, 2 * LANES), lambda hp, bb, i: (bb, i, hp)),
        out_shape=jax.ShapeDtypeStruct((b, s, DIFF_HEADS * DIFF_V_DIM), jnp.bfloat16),
        scratch_shapes=[
            pltpu.VMEM((4, nkc, tq, DIFF_TK), jnp.float32),
            pltpu.VMEM((4, tq, LANES), jnp.float32),
            pltpu.VMEM((2, s, 2 * LANES), jnp.bfloat16),
            pltpu.VMEM((4, tq, 2 * LANES), jnp.float32),
            pltpu.VMEM((4, 8, LANES), jnp.float32),
        ],
        compiler_params=_params(3),
        name="diff_attn",
    )(zd, zd, zd, zd, zd, bias_tiles, bias_max, lam_p)


def _dil_kernel(a_ref, *refs, nb, r, length, width):
    w_refs, s_refs = refs[:nb], refs[nb:2 * nb]
    bias_ref, o_ref, lse_ref, wbf_scr, z_ref = refs[2 * nb:2 * nb + 5]

    @pl.when(pl.program_id(0) == 0)
    def _():
        _cast_weight_blocks(w_refs, s_refs, wbf_scr)

    _project_by_residue(a_ref, wbf_scr, z_ref, refs[2 * nb + 5:], r)

    nqb = length // DIL_TQ
    lane = lax.broadcasted_iota(jnp.int32, (DIL_TQ, LANES), 1)

    def block_coords(t):
        if nqb == 1:
            return t, 0
        if r == 1:
            return 0, t
        return t // nqb, t % nqb

    def body(tt, carry):
        staged = []
        for i in range(DIL_UNROLL):
            c, qb = block_coords(tt * DIL_UNROLL + i)
            q0 = pl.multiple_of(qb * DIL_TQ, DIL_TQ)
            ws = pl.multiple_of(jnp.clip(q0 - DIL_HALF, 0, length - width), DIL_HALF)
            var = jnp.where(qb == 0, 0, jnp.where(qb == nqb - 1, 2, 1))
            scs = []
            for h in range(DIL_HEADS):
                col = h * HEAD_DIM
                q = z_ref[c, pl.ds(q0, DIL_TQ), col:col + HEAD_DIM]
                kw = z_ref[c, pl.ds(ws, width), col + DIL_WIDTH:col + DIL_WIDTH + HEAD_DIM]
                scs.append(_dot_nt(q, kw) + bias_ref[var, h])
            staged.append((c, q0, ws, scs))
        for c, q0, ws, scs in staged:
            out_rows = pl.ds(q0 * r + c, DIL_TQ, stride=r) if r > 1 else pl.ds(q0, DIL_TQ)
            lse_tile = jnp.zeros((DIL_TQ, LANES), jnp.float32)
            for h in range(DIL_HEADS):
                col = 2 * DIL_WIDTH + h * HEAD_DIM
                vw = z_ref[c, pl.ds(ws, width), col:col + HEAD_DIM]
                sc = scs[h]
                mrow = jnp.max(sc, axis=-1, keepdims=True)
                e = jnp.exp2(sc - mrow)
                den = jnp.sum(e, axis=-1, keepdims=True)
                o_ref[h, out_rows, :] = jnp.dot(e.astype(vw.dtype), vw,
                                                preferred_element_type=jnp.float32) / den
                lse_tile = jnp.where(lane == h, mrow + jnp.log2(den), lse_tile)
            lse_ref[out_rows, :] = lse_tile
        return carry
    lax.fori_loop(0, r * nqb // DIL_UNROLL, body, 0)


def _dil_attention(h, w_in, scales, layer, g, bias_tiles, batch):
    m, k = h.shape
    s = m // batch
    _, r = DIL_GROUPS[g]
    length = s // r
    nb = 3
    n = nb * W_BLOCK
    width = bias_tiles.shape[-1]
    spec = _weight_specs(layer, lambda j, t: (OFF_LQ, OFF_LK, OFF_LV)[t] // W_BLOCK + g, 1)
    kern = functools.partial(_dil_kernel, nb=nb, r=r, length=length, width=width)
    return pl.pallas_call(
        kern,
        grid=(batch,),
        in_specs=([pl.BlockSpec((s, k), lambda bb: (bb, 0))]
                  + [spec(t, True) for t in range(nb)] + [spec(t, False) for t in range(nb)]
                  + [pl.BlockSpec(bias_tiles.shape, lambda bb: (0, 0, 0, 0))]),
        out_specs=[pl.BlockSpec((None, DIL_HEADS, s, HEAD_DIM), lambda bb: (bb, 0, 0, 0)),
                   pl.BlockSpec((None, s, LANES), lambda bb: (bb, 0, 0))],
        out_shape=[jax.ShapeDtypeStruct((batch, DIL_HEADS, s, HEAD_DIM), jnp.float32),
                   jax.ShapeDtypeStruct((batch, s, LANES), jnp.float32)],
        scratch_shapes=([pltpu.VMEM((k, n), jnp.bfloat16), pltpu.VMEM((r, length, n), jnp.bfloat16)]
                        + _residue_scratch(n, s, r)),
        compiler_params=_params(1),
        name=f"dil_attn_g{g}",
    )(h, *([w_in] * nb), *([scales] * nb), bias_tiles)


def _finish_kernel(x_ref, oa_ref, ob0_ref, ob1_ref, ob2_ref, l0_ref, l1_ref, l2_ref, dg_ref, h_ref, wg_ref,
                   mkv_ref, wa_ref, wb_ref, wm_ref, wo_ref, g_ref, *refs, final):
    out_refs, zg_ref = refs[:-1], refs[-1]
    bf16 = jnp.bfloat16
    f32 = jnp.float32

    pending = list(range(wg_ref.shape[1] // W_BLOCK))

    def project(n):
        for _ in range(n):
            cb = pending.pop(0)
            cols = slice(cb * W_BLOCK, (cb + 1) * W_BLOCK)
            zg_ref[:, cols] = jnp.dot(h_ref[...], wg_ref[:, cols], preferred_element_type=f32).astype(zg_ref.dtype)

    def half_silu(zh):
        zh = zh.astype(f32)
        return zh + zh * jnp.tanh(zh)

    c_lg, c_mq, c_mg, c_gate = 0, DIL_WIDTH, DIL_WIDTH + MEM_WIDTH, DIL_WIDTH + 2 * MEM_WIDTH

    project(3)

    ya = jnp.dot((oa_ref[...].astype(f32) * half_silu(dg_ref[...])).astype(bf16), wa_ref[...],
                 preferred_element_type=jnp.float32)
    project(2)

    l0, l1, l2 = l0_ref[...], l1_ref[...], l2_ref[...]
    lmax = jnp.maximum(jnp.maximum(l0, l1), l2)
    w0, w1, w2 = jnp.exp2(l0 - lmax), jnp.exp2(l1 - lmax), jnp.exp2(l2 - lmax)
    inv = 1.0 / (w0 + w1 + w2)
    w0, w1, w2 = w0 * inv, w1 * inv, w2 * inv
    ob_refs = (ob0_ref, ob1_ref, ob2_ref)
    parts = []
    for h in range(DIL_HEADS):
        acc = None
        for wg, ob_ref in zip((w0, w1, w2), ob_refs):
            term = wg[:, h:h + 1] * ob_ref[h]
            acc = term if acc is None else acc + term
        parts.append(acc)
    ob = jnp.concatenate(parts, axis=-1)
    yb = jnp.dot((ob * half_silu(zg_ref[:, c_lg:c_lg + DIL_WIDTH])).astype(bf16), wb_ref[...],
                 preferred_element_type=jnp.float32)
    project(2)

    scs = [_dot_nt(zg_ref[:, c_mq + h * HEAD_DIM:c_mq + (h + 1) * HEAD_DIM],
                   mkv_ref[:, h * HEAD_DIM:(h + 1) * HEAD_DIM]) for h in range(MEM_HEADS)]
    project(1)
    parts = []
    for h in range(MEM_HEADS):
        sc = scs[h]
        e = jnp.exp2(sc - jnp.max(sc, axis=-1, keepdims=True))
        den = jnp.sum(e, axis=-1, keepdims=True)
        vcols = slice(MEM_WIDTH + h * HEAD_DIM, MEM_WIDTH + (h + 1) * HEAD_DIM)
        parts.append(jnp.dot(e.astype(bf16), mkv_ref[:, vcols], preferred_element_type=jnp.float32) / den)
    om = jnp.concatenate(parts, axis=-1)
    project(1)
    ym = jnp.dot((om * half_silu(zg_ref[:, c_mg:c_mg + MEM_WIDTH])).astype(bf16), wm_ref[...],
                 preferred_element_type=jnp.float32)
    project(len(pending))

    merged = (ya + yb + ym
              + jnp.tanh(zg_ref[:, c_gate:c_gate + D_MODEL].astype(f32)) * ya
              + jnp.tanh(zg_ref[:, c_gate + D_MODEL:c_gate + 2 * D_MODEL].astype(f32)) * yb
              + jnp.tanh(zg_ref[:, c_gate + 2 * D_MODEL:c_gate + 3 * D_MODEL].astype(f32)) * ym)
    xn = x_ref[...] + jnp.dot(merged.astype(bf16), wo_ref[...], preferred_element_type=jnp.float32)
    hn = xn * lax.rsqrt(jnp.mean(xn * xn, axis=-1, keepdims=True) + EPS) * g_ref[...]
    if final:
        out_refs[0][...] = hn
    else:
        out_refs[0][...] = xn
        out_refs[1][...] = hn.astype(out_refs[1].dtype)


def _finish(x, oa, obs, lses, zd, h, wg, mkv, wa, wb, wm, wo, g_next, final):
    b, s, d = x.shape
    t = FIN_T
    row = lambda width: pl.BlockSpec((None, t, width), lambda bb, i: (bb, i, 0))
    full = lambda arr: pl.BlockSpec(arr.shape, lambda bb, i: (0,) * arr.ndim)
    heads = pl.BlockSpec((None, DIL_HEADS, t, HEAD_DIM), lambda bb, i: (bb, 0, i, 0))
    in_specs = [row(d), row(d), heads, heads, heads,
                row(LANES), row(LANES), row(LANES),
                pl.BlockSpec((None, t, d), lambda bb, i: (bb, i, OFF_DG // D_MODEL)),
                row(d), full(wg),
                pl.BlockSpec((None, N_MEM, 2 * MEM_WIDTH), lambda bb, i: (bb, 0, 0)),
                full(wa), full(wb), full(wm), full(wo),
                pl.BlockSpec((1, d), lambda bb, i: (0, 0))]
    if final:
        out_specs = [row(d)]
        out_shape = [jax.ShapeDtypeStruct((b, s, d), jnp.float32)]
    else:
        out_specs = [row(d), row(d)]
        out_shape = [jax.ShapeDtypeStruct((b, s, d), jnp.float32),
                     jax.ShapeDtypeStruct((b, s, d), jnp.bfloat16)]
    return pl.pallas_call(
        functools.partial(_finish_kernel, final=final),
        grid=(b, s // t),
        in_specs=in_specs,
        out_specs=out_specs,
        out_shape=out_shape,
        scratch_shapes=[pltpu.VMEM((t, wg.shape[1]), jnp.bfloat16)],
        compiler_params=_params(2),
        name="finish",
    )(x, oa, *obs, *lses, zd, h, wg, mkv, wa, wb, wm, wo, g_next.reshape(1, d))


def kernel(x, mem, g_norm, w_in, diff_lambda, w_mem_kv, g_mem, w_br_diff, w_br_dil, w_br_mem, w_out,
           rel_bias, g_final):
    b, s, d = x.shape
    depth = w_in.shape[0]
    bf16 = jnp.bfloat16
    m_rows = b * s

    diff_tiles, diff_bias_max = _diff_bias_tiles(rel_bias, s)
    dil_tiles = [_dil_bias_tiles(rel_bias, g, s) for g in range(len(DIL_GROUPS))]
    mkv_all = _mem_kv(mem.reshape(b * N_MEM, d), g_mem, w_mem_kv)

    col_scale = np.ones((1, N_IN), np.float32)
    col_scale[0, OFF_DQ:OFF_DK] = DIFF_QK_DIM ** -0.5 * LOG2E
    col_scale[0, OFF_LQ:OFF_LK] = HEAD_DIM ** -0.5 * LOG2E
    col_scale[0, OFF_MQ:OFF_MG] = HEAD_DIM ** -0.5 * LOG2E
    col_scale[0, OFF_DG:OFF_LQ] = 0.5
    col_scale[0, OFF_LG:OFF_MQ] = 0.5
    col_scale[0, OFF_MG:] = 0.5
    col_scale = jnp.asarray(col_scale)

    h = _rmsnorm(x.reshape(m_rows, d), g_norm[0], bf16)
    out = None
    for l in range(depth):
        zd = _project(h, w_in, col_scale, l, lambda j, t: 4 * j + t, 4, OFF_LQ // (4 * W_BLOCK),
                      bf16).reshape(b, s, -1)
        wg = _cast_cols(w_in, col_scale, l, OFF_LG // W_BLOCK, (N_IN - OFF_LG) // W_BLOCK)
        mkv = mkv_all[l].reshape(b, N_MEM, 2 * MEM_WIDTH)
        lam_init = 0.8 - 0.6 * math.exp(-0.3 * l)
        oa = _diff_attention(zd, diff_tiles, diff_bias_max, diff_lambda[l], lam_init)
        obs, lses = zip(*[_dil_attention(h, w_in, col_scale, l, g, dil_tiles[g], b)
                          for g in range(len(DIL_GROUPS))])

        final = l == depth - 1
        g_next = g_final if final else g_norm[l + 1]
        res = _finish(x, oa, obs, lses, zd, h.reshape(b, s, d), wg, mkv,
                      w_br_diff[l].astype(bf16), w_br_dil[l].astype(bf16),
                      w_br_mem[l].astype(bf16), (0.5 * w_out[l]).astype(bf16), g_next, final)
        if final:
            out = res[0]
        else:
            x, h3 = res
            h = h3.reshape(m_rows, d)
    return out
```

```python
import functools
import math

import jax
import jax.numpy as jnp
import numpy as np
from jax import lax
from jax.experimental import pallas as pl
from jax.experimental.pallas import tpu as pltpu

D_MODEL = 1024
N_MEM = 256
EPS = 1e-6
NEG_INF = -1e30

DIFF_HEADS = 8
DIFF_QK_DIM = 64
DIFF_V_DIM = 128
DIL_GROUPS = ((128, 1), (512, 4), (2048, 16))
DIL_HEADS = 4
HEAD_DIM = 128
DIL_WIDTH = DIL_HEADS * HEAD_DIM
DIL_HALF = 64
MEM_HEADS = 4
MEM_WIDTH = MEM_HEADS * HEAD_DIM
REL_BUCKETS = 32
REL_MAX_DIST = 1024

OFF_DQ, OFF_DK, OFF_DV, OFF_DG = 0, 1024, 2048, 3072
OFF_LQ, OFF_LK, OFF_LV, OFF_LG = 4096, 5632, 7168, 8704
OFF_MQ, OFF_MG, OFF_MGATE = 9216, 9728, 10240
N_IN = 13312

LOG2E = 1.4426950408889634

LANES = 128
MXU_EDGE = 256
VMEM_LIMIT_BYTES = 56 * 1024 * 1024

DIFF_TQ = 1024
DIFF_TK = MXU_EDGE
DIFF_SKEW = 2
DIFF_BOUND_SLACK = 1.0 + 2.0 ** -8
DIFF_MIN_LOG2_SUM = -80.0
DIL_TQ = 128
DIL_UNROLL = 4
FIN_T = 512
MM_TM = 2048
MM_SUB = 512
W_BLOCK = 512
STRIDE_STEP = 4


def _params(n_grid_dims):
    return pltpu.CompilerParams(dimension_semantics=("arbitrary",) * n_grid_dims,
                                vmem_limit_bytes=VMEM_LIMIT_BYTES)


def _dot_nt(a, b):
    return lax.dot_general(a, b, (((1,), (1,)), ((), ())), preferred_element_type=jnp.float32)


def _rms_kernel(x_ref, g_ref, o_ref):
    x = x_ref[...]
    ms = jnp.mean(x * x, axis=-1, keepdims=True)
    o_ref[...] = (x * lax.rsqrt(ms + EPS) * g_ref[...]).astype(o_ref.dtype)


def _rmsnorm(x2d, g, out_dtype, tm=512):
    m, d = x2d.shape
    tm = min(tm, m)
    return pl.pallas_call(
        _rms_kernel,
        grid=(m // tm,),
        in_specs=[pl.BlockSpec((tm, d), lambda i: (i, 0)),
                  pl.BlockSpec((1, d), lambda i: (0, 0))],
        out_specs=pl.BlockSpec((tm, d), lambda i: (i, 0)),
        out_shape=jax.ShapeDtypeStruct((m, d), out_dtype),
        compiler_params=_params(1),
        name="rmsnorm",
    )(x2d, g.reshape(1, d))


def _cast_weight_blocks(w_refs, s_refs, wbf_scr):
    for t, (w_ref, s_ref) in enumerate(zip(w_refs, s_refs)):
        wbf_scr[:, t * W_BLOCK:(t + 1) * W_BLOCK] = (w_ref[...] * s_ref[...]).astype(wbf_scr.dtype)


def _proj_kernel(a_ref, *refs, nb):
    w_refs, s_refs, o_ref, wbf_scr = refs[:nb], refs[nb:2 * nb], refs[2 * nb], refs[2 * nb + 1]

    @pl.when(pl.program_id(1) == 0)
    def _():
        _cast_weight_blocks(w_refs, s_refs, wbf_scr)

    sub = min(MM_SUB, a_ref.shape[0])

    def body(s, carry):
        r0 = pl.multiple_of(s * sub, sub)
        acc = jnp.dot(a_ref[pl.ds(r0, sub), :], wbf_scr[...], preferred_element_type=jnp.float32)
        o_ref[pl.ds(r0, sub), :] = acc.astype(o_ref.dtype)
        return carry
    lax.fori_loop(0, a_ref.shape[0] // sub, body, 0, unroll=True)


def _weight_specs(layer, blocks, n_grid):
    def spec(t, weight):
        if n_grid == 2:
            imap = (lambda j, i: (layer, 0, blocks(j, t))) if weight else (lambda j, i: (0, blocks(j, t)))
        else:
            imap = (lambda i: (layer, 0, blocks(0, t))) if weight else (lambda i: (0, blocks(0, t)))
        return pl.BlockSpec((None, D_MODEL, W_BLOCK) if weight else (1, W_BLOCK), imap)
    return spec


def _project(a, w_in, scales, layer, blocks, nb, n_tiles, out_dtype):
    m, k = a.shape
    tm = min(MM_TM, m)
    tn = nb * W_BLOCK
    spec = _weight_specs(layer, blocks, 2)
    return pl.pallas_call(
        functools.partial(_proj_kernel, nb=nb),
        grid=(n_tiles, m // tm),
        in_specs=([pl.BlockSpec((tm, k), lambda j, i: (i, 0))]
                  + [spec(t, True) for t in range(nb)] + [spec(t, False) for t in range(nb)]),
        out_specs=pl.BlockSpec((tm, tn), lambda j, i: (i, j)),
        out_shape=jax.ShapeDtypeStruct((m, n_tiles * tn), out_dtype),
        scratch_shapes=[pltpu.VMEM((k, tn), jnp.bfloat16)],
        compiler_params=_params(2),
        name="in_proj",
    )(a, *([w_in] * nb), *([scales] * nb))


def _project_by_residue(a_ref, wbf_scr, o_ref, scrs, r):
    if r > STRIDE_STEP:
        scrs, tmp = scrs[:-1], scrs[-1]

    rows = a_ref.shape[0]
    n_slabs = wbf_scr.shape[1] // LANES
    sub = min(MM_SUB, rows)
    n_sub = rows // sub
    piece = sub // r

    def matmul_step(t):
        acc = jnp.dot(a_ref[t * sub:(t + 1) * sub, :], wbf_scr[...], preferred_element_type=jnp.float32)
        if r == 1:
            o_ref[0, t * sub:(t + 1) * sub, :] = acc.astype(o_ref.dtype)
        else:
            for k in range(n_slabs):
                scrs[t][k] = acc[:, k * LANES:(k + 1) * LANES]

    def relayout_step(t):
        out_rows = slice(t * piece, (t + 1) * piece)
        for k in range(n_slabs):
            cols = slice(k * LANES, (k + 1) * LANES)
            if r > STRIDE_STEP:
                r2 = r // STRIDE_STEP
                for c1 in range(STRIDE_STEP):
                    tmp[k, c1] = scrs[t][k, pl.ds(c1, sub // STRIDE_STEP, stride=STRIDE_STEP), :]
                for c1 in range(STRIDE_STEP):
                    for c2 in range(r2):
                        o_ref[c1 + STRIDE_STEP * c2, out_rows, cols] = (
                            tmp[k, c1, pl.ds(c2, piece, stride=r2), :].astype(o_ref.dtype))
            else:
                for c in range(r):
                    o_ref[c, out_rows, cols] = scrs[t][k, pl.ds(c, piece, stride=r), :].astype(o_ref.dtype)

    for t in range(n_sub + 1):
        if t < n_sub:
            matmul_step(t)
        if r > 1 and t > 0:
            relayout_step(t - 1)


def _residue_scratch(n, seq, r):
    sub = min(MM_SUB, seq)
    return (([pltpu.VMEM((n // LANES, sub, LANES), jnp.float32) for _ in range(seq // sub)] if r > 1 else [])
            + ([pltpu.VMEM((n // LANES, STRIDE_STEP, sub // STRIDE_STEP, LANES), jnp.float32)]
               if r > STRIDE_STEP else []))


def _cast_cols_kernel(w_ref, s_ref, o_ref):
    o_ref[...] = (w_ref[...] * s_ref[...]).astype(o_ref.dtype)


def _cast_cols(w_in, scales, layer, first_block, n_blocks):
    d = w_in.shape[1]
    return pl.pallas_call(
        _cast_cols_kernel,
        grid=(n_blocks,),
        in_specs=[pl.BlockSpec((None, d, W_BLOCK), lambda j: (layer, 0, first_block + j)),
                  pl.BlockSpec((1, W_BLOCK), lambda j: (0, first_block + j))],
        out_specs=pl.BlockSpec((d, W_BLOCK), lambda j: (0, j)),
        out_shape=jax.ShapeDtypeStruct((d, n_blocks * W_BLOCK), jnp.bfloat16),
        compiler_params=_params(1),
        name="cast_cols",
    )(w_in, scales)


def _mem_kv_kernel(mem_ref, g_ref, w_ref, o_ref):
    x = mem_ref[...]
    ms = jnp.mean(x * x, axis=-1, keepdims=True)
    mem_n = (x * lax.rsqrt(ms + EPS) * g_ref[...]).astype(jnp.bfloat16)
    o_ref[...] = jnp.dot(mem_n, w_ref[...].astype(jnp.bfloat16),
                         preferred_element_type=jnp.float32).astype(o_ref.dtype)


def _mem_kv(mem2d, g_mem, w_mem_kv):
    rows, d = mem2d.shape
    depth, _, n = w_mem_kv.shape
    return pl.pallas_call(
        _mem_kv_kernel,
        grid=(depth,),
        in_specs=[pl.BlockSpec((rows, d), lambda l: (0, 0)),
                  pl.BlockSpec((None, 1, d), lambda l: (l, 0, 0)),
                  pl.BlockSpec((None, d, n), lambda l: (l, 0, 0))],
        out_specs=pl.BlockSpec((None, rows, n), lambda l: (l, 0, 0)),
        out_shape=jax.ShapeDtypeStruct((depth, rows, n), jnp.bfloat16),
        compiler_params=_params(1),
        name="mem_kv",
    )(mem2d, g_mem.reshape(depth, 1, d), w_mem_kv)


def _t5_bucket(rel):
    half = REL_BUCKETS // 2
    max_exact = half // 2
    ret = jnp.where(rel > 0, half, 0)
    n = jnp.abs(rel)
    nf = jnp.maximum(n, 1).astype(jnp.float32)
    large = max_exact + (jnp.log(nf / max_exact) / math.log(REL_MAX_DIST / max_exact)
                         * (half - max_exact)).astype(jnp.int32)
    large = jnp.minimum(large, half - 1)
    return ret + jnp.where(n < max_exact, n, large)


def _toeplitz_kernel(u_ref, o_ref):
    n_tiles, rows, tile_w = o_ref.shape
    x = jnp.broadcast_to(u_ref[...], (rows, u_ref.shape[-1]))
    y = pltpu.roll(x, 0, 1, stride=1, stride_axis=0)
    for d in range(n_tiles):
        o_ref[d] = y[:, d * tile_w:(d + 1) * tile_w]


def _toeplitz_tiles(vals, rows, n_tiles, tile_w):
    groups = vals.shape[0]
    cols = n_tiles * tile_w
    period = pl.next_power_of_2(rows + cols - 1)
    pad = jnp.zeros((groups, period - (rows + cols - 1)), vals.dtype)
    u = jnp.concatenate([vals[:, rows - 1:], pad, vals[:, :rows - 1]], axis=1).reshape(groups, 1, period)
    return pl.pallas_call(
        _toeplitz_kernel,
        grid=(groups,),
        in_specs=[pl.BlockSpec((None, 1, period), lambda g: (g, 0, 0))],
        out_specs=pl.BlockSpec((None, n_tiles, rows, tile_w), lambda g: (g, 0, 0, 0)),
        out_shape=jax.ShapeDtypeStruct((groups, n_tiles, rows, tile_w), vals.dtype),
        compiler_params=_params(1),
        name="toeplitz_tiles",
    )(u)


def _diff_bias_tiles(rel_bias, seq):
    nd = seq // DIFF_TK - 1
    rel = jnp.arange(2 * seq - 1, dtype=jnp.int32) - (seq - 1)
    tvec = jnp.take(rel_bias[:, :DIFF_HEADS], _t5_bucket(rel), axis=0).T * LOG2E
    bias_max = jnp.broadcast_to(jnp.max(tvec, axis=1)[:, None, None], (DIFF_HEADS, 8, LANES))
    return _toeplitz_tiles(tvec, DIFF_TK, 2 * nd + 1, DIFF_TK), bias_max


def _dil_bias_tiles(rel_bias, g, seq):
    _, r = DIL_GROUPS[g]
    length = seq // r
    width = min(2 * DIL_TQ, length)
    shifts = jnp.array([0, -DIL_HALF, -(width - DIL_TQ)], dtype=jnp.int32)
    delta = shifts[:, None] + jnp.arange(DIL_TQ + width - 1, dtype=jnp.int32)[None, :] - (DIL_TQ - 1)
    c0 = DIFF_HEADS + g * DIL_HEADS
    bias = jnp.take(rel_bias[:, c0:c0 + DIL_HEADS], _t5_bucket(delta * r), axis=0) * LOG2E
    bias = jnp.where((jnp.abs(delta) <= DIL_HALF)[..., None], bias, NEG_INF)
    vals = jnp.transpose(bias, (0, 2, 1)).reshape(3 * DIL_HEADS, DIL_TQ + width - 1)
    return _toeplitz_tiles(vals, DIL_TQ, 1, width).reshape(3, DIL_HEADS, DIL_TQ, width)


def _diff_kernel(q0_ref, q1_ref, k0_ref, k1_ref, v_ref, bias_ref, bmax_ref, lam_ref, o_ref,
                 s_scr, mx_scr, vext_scr, acc_scr, kmax_scr, *, nkc):
    i = pl.program_id(2)
    tq = q0_ref.shape[0]
    nrb = tq // DIFF_TK

    @pl.when(i == 0)
    def _():
        for a in range(2):
            vext_scr[a, :, 0:LANES] = v_ref[:, a * LANES:(a + 1) * LANES]
            vext_scr[a, :, LANES:2 * LANES] = jnp.ones((vext_scr.shape[1], LANES), vext_scr.dtype)
        hlane = lax.broadcasted_iota(jnp.int32, (1, LANES), 1)
        for m, k_ref in enumerate((k0_ref, k1_ref)):
            col_max = jnp.max(jnp.square(k_ref[...].astype(jnp.float32)), axis=0, keepdims=True)
            for a in range(2):
                in_head = (hlane >= a * DIFF_QK_DIM) & (hlane < (a + 1) * DIFF_QK_DIM)
                kmax = jnp.sqrt(jnp.sum(jnp.where(in_head, col_max, 0.0), axis=-1, keepdims=True))
                kmax_scr[2 * a + m] = jnp.broadcast_to(kmax, kmax_scr.shape[1:])

    lp = lam_ref[...]
    lam_init = lp[4:5, 0:1]
    lam = (jnp.exp(jnp.sum(lp[0:1] * lp[1:2], axis=-1, keepdims=True))
           - jnp.exp(jnp.sum(lp[2:3] * lp[3:4], axis=-1, keepdims=True)) + lam_init)

    lane = lax.broadcasted_iota(jnp.int32, (tq, LANES), 1)
    q_refs = (q0_ref, q1_ref)
    k_refs = (k0_ref, k1_ref)

    def masked_q(u):
        a, m = divmod(u, 2)
        head_lanes = (lane >= a * DIFF_QK_DIM) & (lane < (a + 1) * DIFF_QK_DIM)
        q = q_refs[m][...]
        return jnp.where(head_lanes, q, jnp.zeros_like(q))

    def qk_chunk(u, qm, j):
        a, m = divmod(u, 2)
        s = _dot_nt(qm, k_refs[m][j * DIFF_TK:(j + 1) * DIFF_TK, :])
        for rb in range(nrb):
            rows = slice(rb * DIFF_TK, (rb + 1) * DIFF_TK)
            sb = s[rows] + bias_ref[a, j - nrb * i - rb + (nkc - 1)]
            s_scr[u % 2, j, rows, :] = sb
            mtile = jnp.maximum(sb[:, :LANES], sb[:, LANES:])
            if j == 0:
                mx_scr[u, rows, :] = mtile
            else:
                mx_scr[u, rows, :] = jnp.maximum(mx_scr[u, rows, :], mtile)

    def row_max(u):
        return jnp.broadcast_to(jnp.max(mx_scr[u], axis=-1, keepdims=True), (tq, DIFF_TK))

    def pv_chunk(u, mb, j, acc):
        e = jnp.exp2(s_scr[u % 2, j] - mb).astype(vext_scr.dtype)
        part = jnp.dot(e, vext_scr[u // 2, j * DIFF_TK:(j + 1) * DIFF_TK, :],
                       preferred_element_type=jnp.float32)
        return part if acc is None else acc + part

    def finish_head(a):
        n0 = acc_scr[2 * a]
        n1 = acc_scr[2 * a + 1]
        out = n0[:, :LANES] / n0[:, LANES:] - lam * (n1[:, :LANES] / n1[:, LANES:])
        ms = jnp.mean(out * out, axis=-1, keepdims=True)
        o_ref[:, a * LANES:(a + 1) * LANES] = (out * lax.rsqrt(ms + EPS) * (1.0 - lam_init)).astype(o_ref.dtype)

    n_units = 4

    def row_bound(u):
        a, m = divmod(u, 2)
        head_lanes = (lane >= a * DIFF_QK_DIM) & (lane < (a + 1) * DIFF_QK_DIM)
        qsq = jnp.square(q_refs[m][...].astype(jnp.float32))
        qn = jnp.sqrt(jnp.sum(jnp.where(head_lanes, qsq, 0.0), axis=-1, keepdims=True))
        ub = qn * (kmax_scr[u][0:1, 0:1] * DIFF_BOUND_SLACK) + bmax_ref[a][0:1, 0:1]
        return jnp.broadcast_to(ub, (tq, DIFF_TK))

    items = [(u, j) for u in range(n_units) for j in range(nkc)]
    qms, ubs, logits, accs = {}, {}, {}, {}
    min_sum = None
    for n in range(len(items) + DIFF_SKEW):
        if n < len(items):
            u, j = items[n]
            if j == 0:
                qms[u] = masked_q(u)
                ubs[u] = row_bound(u)
            a, m = divmod(u, 2)
            logits[n] = _dot_nt(qms[u], k_refs[m][j * DIFF_TK:(j + 1) * DIFF_TK, :])
        if n >= DIFF_SKEW:
            u, j = items[n - DIFF_SKEW]
            a = u // 2
            s = logits.pop(n - DIFF_SKEW)
            e = jnp.concatenate(
                [jnp.exp2(s[rb * DIFF_TK:(rb + 1) * DIFF_TK] + bias_ref[a, j - nrb * i - rb + (nkc - 1)]
                          - ubs[u][rb * DIFF_TK:(rb + 1) * DIFF_TK]) for rb in range(nrb)],
                axis=0).astype(vext_scr.dtype)
            part = jnp.dot(e, vext_scr[a, j * DIFF_TK:(j + 1) * DIFF_TK, :], preferred_element_type=jnp.float32)
            accs[u] = part if j == 0 else accs[u] + part
            if j == nkc - 1:
                acc = accs.pop(u)
                acc_scr[u] = acc
                row_sum_min = jnp.min(acc[:, LANES:])
                min_sum = row_sum_min if min_sum is None else jnp.minimum(min_sum, row_sum_min)
                if u % 2 == 1:
                    finish_head(u // 2)

    @pl.when(min_sum < 2.0 ** DIFF_MIN_LOG2_SUM)
    def _():
        mb_prev = None
        for u in range(n_units + 1):
            qm = masked_q(u) if u < n_units else None
            acc = None
            for j in range(nkc):
                if u < n_units:
                    qk_chunk(u, qm, j)
                if u > 0:
                    acc = pv_chunk(u - 1, mb_prev, j, acc)
            if u > 0:
                acc_scr[u - 1] = acc
                if (u - 1) % 2 == 1:
                    finish_head((u - 1) // 2)
            if u < n_units:
                mb_prev = row_max(u)


def _diff_attention(zd, bias_tiles, bias_max, lam_p, lam_init):
    b, s, _ = zd.shape
    tq = DIFF_TQ
    nkc = s // DIFF_TK
    kern = functools.partial(_diff_kernel, nkc=nkc)
    lam_in = jnp.concatenate([lam_p, jnp.full((1, DIFF_QK_DIM), lam_init, lam_p.dtype)], axis=0)
    kblk = OFF_DK // LANES
    vblk = OFF_DV // (2 * LANES)
    return pl.pallas_call(
        kern,
        grid=(DIFF_HEADS // 2, b, s // tq),
        in_specs=[
            pl.BlockSpec((None, tq, LANES), lambda hp, bb, i: (bb, i, hp)),
            pl.BlockSpec((None, tq, LANES), lambda hp, bb, i: (bb, i, DIFF_HEADS // 2 + hp)),
            pl.BlockSpec((None, s, LANES), lambda hp, bb, i: (bb, 0, kblk + hp)),
            pl.BlockSpec((None, s, LANES), lambda hp, bb, i: (bb, 0, kblk + DIFF_HEADS // 2 + hp)),
            pl.BlockSpec((None, s, 2 * LANES), lambda hp, bb, i: (bb, 0, vblk + hp)),
            pl.BlockSpec((2, 2 * nkc - 1, DIFF_TK, DIFF_TK), lambda hp, bb, i: (hp, 0, 0, 0)),
            pl.BlockSpec((2, 8, LANES), lambda hp, bb, i: (hp, 0, 0)),
            pl.BlockSpec((5, DIFF_QK_DIM), lambda hp, bb, i: (0, 0)),
        ],
        out_specs=pl.BlockSpec((None, tq, 2 * LANES), lambda hp, bb, i: (bb, i, hp)),
        out_shape=jax.ShapeDtypeStruct((b, s, DIFF_HEADS * DIFF_V_DIM), jnp.bfloat16),
        scratch_shapes=[
            pltpu.VMEM((2, nkc, tq, DIFF_TK), jnp.float32),
            pltpu.VMEM((4, tq, LANES), jnp.float32),
            pltpu.VMEM((2, s, 2 * LANES), jnp.bfloat16),
            pltpu.VMEM((4, tq, 2 * LANES), jnp.float32),
            pltpu.VMEM((4, 8, LANES), jnp.float32),
        ],
        compiler_params=_params(3),
        name="diff_attn",
    )(zd, zd, zd, zd, zd, bias_tiles, bias_max, lam_in)


def _dil_kernel(a_ref, *refs, nb, r, length, width):
    w_refs, s_refs = refs[:nb], refs[nb:2 * nb]
    bias_ref, o_ref, lse_ref, wbf_scr, z_ref = refs[2 * nb:2 * nb + 5]

    @pl.when(pl.program_id(0) == 0)
    def _():
        _cast_weight_blocks(w_refs, s_refs, wbf_scr)

    _project_by_residue(a_ref, wbf_scr, z_ref, refs[2 * nb + 5:], r)

    nqb = length // DIL_TQ
    lane = lax.broadcasted_iota(jnp.int32, (DIL_TQ, LANES), 1)

    def block_coords(t):
        if nqb == 1:
            return t, 0
        if r == 1:
            return 0, t
        return t // nqb, t % nqb

    def body(tt, carry):
        staged = []
        for i in range(DIL_UNROLL):
            c, qb = block_coords(tt * DIL_UNROLL + i)
            q0 = pl.multiple_of(qb * DIL_TQ, DIL_TQ)
            ws = pl.multiple_of(jnp.clip(q0 - DIL_HALF, 0, length - width), DIL_HALF)
            var = jnp.where(qb == 0, 0, jnp.where(qb == nqb - 1, 2, 1))
            scs = []
            for h in range(DIL_HEADS):
                col = h * HEAD_DIM
                q = z_ref[c, pl.ds(q0, DIL_TQ), col:col + HEAD_DIM]
                kw = z_ref[c, pl.ds(ws, width), col + DIL_WIDTH:col + DIL_WIDTH + HEAD_DIM]
                scs.append(_dot_nt(q, kw) + bias_ref[var, h])
            staged.append((c, q0, ws, scs))
        for c, q0, ws, scs in staged:
            out_rows = pl.ds(q0 * r + c, DIL_TQ, stride=r) if r > 1 else pl.ds(q0, DIL_TQ)
            lse_tile = jnp.zeros((DIL_TQ, LANES), jnp.float32)
            for h in range(DIL_HEADS):
                col = 2 * DIL_WIDTH + h * HEAD_DIM
                vw = z_ref[c, pl.ds(ws, width), col:col + HEAD_DIM]
                sc = scs[h]
                mrow = jnp.max(sc, axis=-1, keepdims=True)
                e = jnp.exp2(sc - mrow)
                den = jnp.sum(e, axis=-1, keepdims=True)
                o_ref[h, out_rows, :] = jnp.dot(e.astype(vw.dtype), vw,
                                                preferred_element_type=jnp.float32) / den
                lse_tile = jnp.where(lane == h, mrow + jnp.log2(den), lse_tile)
            lse_ref[out_rows, :] = lse_tile
        return carry
    lax.fori_loop(0, r * nqb // DIL_UNROLL, body, 0)


def _dil_attention(h, w_in, scales, layer, g, bias_tiles, batch):
    m, k = h.shape
    s = m // batch
    _, r = DIL_GROUPS[g]
    length = s // r
    nb = 3
    n = nb * W_BLOCK
    width = bias_tiles.shape[-1]
    spec = _weight_specs(layer, lambda j, t: (OFF_LQ, OFF_LK, OFF_LV)[t] // W_BLOCK + g, 1)
    kern = functools.partial(_dil_kernel, nb=nb, r=r, length=length, width=width)
    return pl.pallas_call(
        kern,
        grid=(batch,),
        in_specs=([pl.BlockSpec((s, k), lambda bb: (bb, 0))]
                  + [spec(t, True) for t in range(nb)] + [spec(t, False) for t in range(nb)]
                  + [pl.BlockSpec(bias_tiles.shape, lambda bb: (0, 0, 0, 0))]),
        out_specs=[pl.BlockSpec((None, DIL_HEADS, s, HEAD_DIM), lambda bb: (bb, 0, 0, 0)),
                   pl.BlockSpec((None, s, LANES), lambda bb: (bb, 0, 0))],
        out_shape=[jax.ShapeDtypeStruct((batch, DIL_HEADS, s, HEAD_DIM), jnp.float32),
                   jax.ShapeDtypeStruct((batch, s, LANES), jnp.float32)],
        scratch_shapes=([pltpu.VMEM((k, n), jnp.bfloat16), pltpu.VMEM((r, length, n), jnp.bfloat16)]
                        + _residue_scratch(n, s, r)),
        compiler_params=_params(1),
        name=f"dil_attn_g{g}",
    )(h, *([w_in] * nb), *([scales] * nb), bias_tiles)


def _finish_kernel(x_ref, oa_ref, ob0_ref, ob1_ref, ob2_ref, l0_ref, l1_ref, l2_ref, dg_ref, h_ref, wg_ref,
                   mkv_ref, wa_ref, wb_ref, wm_ref, wo_ref, g_ref, *refs, final):
    out_refs, zg_ref = refs[:-1], refs[-1]
    bf16 = jnp.bfloat16
    f32 = jnp.float32

    pending = list(range(wg_ref.shape[1] // W_BLOCK))

    def project(n):
        for _ in range(n):
            cb = pending.pop(0)
            cols = slice(cb * W_BLOCK, (cb + 1) * W_BLOCK)
            zg_ref[:, cols] = jnp.dot(h_ref[...], wg_ref[:, cols], preferred_element_type=f32).astype(zg_ref.dtype)

    def half_silu(zh):
        zh = zh.astype(f32)
        return zh + zh * jnp.tanh(zh)

    c_lg, c_mq, c_mg, c_gate = 0, DIL_WIDTH, DIL_WIDTH + MEM_WIDTH, DIL_WIDTH + 2 * MEM_WIDTH

    project(3)

    ya = jnp.dot((oa_ref[...].astype(f32) * half_silu(dg_ref[...])).astype(bf16), wa_ref[...],
                 preferred_element_type=jnp.float32)
    project(2)

    l0, l1, l2 = l0_ref[...], l1_ref[...], l2_ref[...]
    lmax = jnp.maximum(jnp.maximum(l0, l1), l2)
    w0, w1, w2 = jnp.exp2(l0 - lmax), jnp.exp2(l1 - lmax), jnp.exp2(l2 - lmax)
    inv = 1.0 / (w0 + w1 + w2)
    w0, w1, w2 = w0 * inv, w1 * inv, w2 * inv
    ob_refs = (ob0_ref, ob1_ref, ob2_ref)
    parts = []
    for h in range(DIL_HEADS):
        acc = None
        for wg, ob_ref in zip((w0, w1, w2), ob_refs):
            term = wg[:, h:h + 1] * ob_ref[h]
            acc = term if acc is None else acc + term
        parts.append(acc)
    ob = jnp.concatenate(parts, axis=-1)
    yb = jnp.dot((ob * half_silu(zg_ref[:, c_lg:c_lg + DIL_WIDTH])).astype(bf16), wb_ref[...],
                 preferred_element_type=jnp.float32)
    project(2)

    scs = [_dot_nt(zg_ref[:, c_mq + h * HEAD_DIM:c_mq + (h + 1) * HEAD_DIM],
                   mkv_ref[:, h * HEAD_DIM:(h + 1) * HEAD_DIM]) for h in range(MEM_HEADS)]
    project(1)
    parts = []
    for h in range(MEM_HEADS):
        sc = scs[h]
        e = jnp.exp2(sc - jnp.max(sc, axis=-1, keepdims=True))
        den = jnp.sum(e, axis=-1, keepdims=True)
        vcols = slice(MEM_WIDTH + h * HEAD_DIM, MEM_WIDTH + (h + 1) * HEAD_DIM)
        parts.append(jnp.dot(e.astype(bf16), mkv_ref[:, vcols], preferred_element_type=jnp.float32) / den)
    om = jnp.concatenate(parts, axis=-1)
    project(1)
    ym = jnp.dot((om * half_silu(zg_ref[:, c_mg:c_mg + MEM_WIDTH])).astype(bf16), wm_ref[...],
                 preferred_element_type=jnp.float32)
    project(len(pending))

    merged = (ya + yb + ym
              + jnp.tanh(zg_ref[:, c_gate:c_gate + D_MODEL].astype(f32)) * ya
              + jnp.tanh(zg_ref[:, c_gate + D_MODEL:c_gate + 2 * D_MODEL].astype(f32)) * yb
              + jnp.tanh(zg_ref[:, c_gate + 2 * D_MODEL:c_gate + 3 * D_MODEL].astype(f32)) * ym)
    xn = x_ref[...] + jnp.dot(merged.astype(bf16), wo_ref[...], preferred_element_type=jnp.float32)
    hn = xn * lax.rsqrt(jnp.mean(xn * xn, axis=-1, keepdims=True) + EPS) * g_ref[...]
    if final:
        out_refs[0][...] = hn
    else:
        out_refs[0][...] = xn
        out_refs[1][...] = hn.astype(out_refs[1].dtype)


def _finish(x, oa, obs, lses, zd, h, wg, mkv, wa, wb, wm, wo, g_next, final):
    b, s, d = x.shape
    t = FIN_T
    row = lambda width: pl.BlockSpec((None, t, width), lambda bb, i: (bb, i, 0))
    full = lambda arr: pl.BlockSpec(arr.shape, lambda bb, i: (0,) * arr.ndim)
    heads = pl.BlockSpec((None, DIL_HEADS, t, HEAD_DIM), lambda bb, i: (bb, 0, i, 0))
    in_specs = [row(d), row(d), heads, heads, heads,
                row(LANES), row(LANES), row(LANES),
                pl.BlockSpec((None, t, d), lambda bb, i: (bb, i, OFF_DG // D_MODEL)),
                row(d), full(wg),
                pl.BlockSpec((None, N_MEM, 2 * MEM_WIDTH), lambda bb, i: (bb, 0, 0)),
                full(wa), full(wb), full(wm), full(wo),
                pl.BlockSpec((1, d), lambda bb, i: (0, 0))]
    if final:
        out_specs = [row(d)]
        out_shape = [jax.ShapeDtypeStruct((b, s, d), jnp.float32)]
    else:
        out_specs = [row(d), row(d)]
        out_shape = [jax.ShapeDtypeStruct((b, s, d), jnp.float32),
                     jax.ShapeDtypeStruct((b, s, d), jnp.bfloat16)]
    return pl.pallas_call(
        functools.partial(_finish_kernel, final=final),
        grid=(b, s // t),
        in_specs=in_specs,
        out_specs=out_specs,
        out_shape=out_shape,
        scratch_shapes=[pltpu.VMEM((t, wg.shape[1]), jnp.bfloat16)],
        compiler_params=_params(2),
        name="finish",
    )(x, oa, *obs, *lses, zd, h, wg, mkv, wa, wb, wm, wo, g_next.reshape(1, d))


def kernel(x, mem, g_norm, w_in, diff_lambda, w_mem_kv, g_mem, w_br_diff, w_br_dil, w_br_mem, w_out,
           rel_bias, g_final):
    b, s, d = x.shape
    depth = w_in.shape[0]
    bf16 = jnp.bfloat16
    m_rows = b * s

    diff_tiles, diff_bias_max = _diff_bias_tiles(rel_bias, s)
    dil_tiles = [_dil_bias_tiles(rel_bias, g, s) for g in range(len(DIL_GROUPS))]
    mkv_all = _mem_kv(mem.reshape(b * N_MEM, d), g_mem, w_mem_kv)

    col_scale = np.ones((1, N_IN), np.float32)
    col_scale[0, OFF_DQ:OFF_DK] = DIFF_QK_DIM ** -0.5 * LOG2E
    col_scale[0, OFF_LQ:OFF_LK] = HEAD_DIM ** -0.5 * LOG2E
    col_scale[0, OFF_MQ:OFF_MG] = HEAD_DIM ** -0.5 * LOG2E
    col_scale[0, OFF_DG:OFF_LQ] = 0.5
    col_scale[0, OFF_LG:OFF_MQ] = 0.5
    col_scale[0, OFF_MG:] = 0.5
    col_scale = jnp.asarray(col_scale)

    h = _rmsnorm(x.reshape(m_rows, d), g_norm[0], bf16)
    out = None
    for l in range(depth):
        zd = _project(h, w_in, col_scale, l, lambda j, t: 4 * j + t, 4, OFF_LQ // (4 * W_BLOCK),
                      bf16).reshape(b, s, -1)
        wg = _cast_cols(w_in, col_scale, l, OFF_LG // W_BLOCK, (N_IN - OFF_LG) // W_BLOCK)
        mkv = mkv_all[l].reshape(b, N_MEM, 2 * MEM_WIDTH)
        lam_init = 0.8 - 0.6 * math.exp(-0.3 * l)
        oa = _diff_attention(zd, diff_tiles, diff_bias_max, diff_lambda[l], lam_init)
        obs, lses = zip(*[_dil_attention(h, w_in, col_scale, l, g, dil_tiles[g], b)
                          for g in range(len(DIL_GROUPS))])

        final = l == depth - 1
        g_next = g_final if final else g_norm[l + 1]
        res = _finish(x, oa, obs, lses, zd, h.reshape(b, s, d), wg, mkv,
                      w_br_diff[l].astype(bf16), w_br_dil[l].astype(bf16),
                      w_br_mem[l].astype(bf16), (0.5 * w_out[l]).astype(bf16), g_next, final)
        if final:
            out = res[0]
        else:
            x, h3 = res
            h = h3.reshape(m_rows, d)
    return out
```

```python
import functools
import math

import jax
import jax.numpy as jnp
import numpy as np
from jax import lax
from jax.experimental import pallas as pl
from jax.experimental.pallas import tpu as pltpu

D_MODEL = 1024
N_MEM = 256
EPS = 1e-6
NEG_INF = -1e30

DIFF_HEADS = 8
DIFF_QK_DIM = 64
DIFF_V_DIM = 128
DIL_GROUPS = ((128, 1), (512, 4), (2048, 16))
DIL_HEADS = 4
HEAD_DIM = 128
DIL_WIDTH = DIL_HEADS * HEAD_DIM
DIL_HALF = 64
MEM_HEADS = 4
MEM_WIDTH = MEM_HEADS * HEAD_DIM
REL_BUCKETS = 32
REL_MAX_DIST = 1024

OFF_DQ, OFF_DK, OFF_DV, OFF_DG = 0, 1024, 2048, 3072
OFF_LQ, OFF_LK, OFF_LV, OFF_LG = 4096, 5632, 7168, 8704
OFF_MQ, OFF_MG, OFF_MGATE = 9216, 9728, 10240
N_IN = 13312

LOG2E = 1.4426950408889634

LANES = 128
MXU_EDGE = 256
VMEM_LIMIT_BYTES = 56 * 1024 * 1024

DIFF_TQ = 1024
DIFF_TK = MXU_EDGE
DIFF_SKEW = 2
DIFF_BOUND_SLACK = 1.0 + 2.0 ** -8
DIFF_MIN_LOG2_SUM = -80.0
DIL_TQ = 128
DIL_UNROLL = 4
FIN_T = 512
MM_TM = 2048
MM_SUB = 512
W_BLOCK = 512
STRIDE_STEP = 4


def _params(n_grid_dims):
    return pltpu.CompilerParams(dimension_semantics=("arbitrary",) * n_grid_dims,
                                vmem_limit_bytes=VMEM_LIMIT_BYTES)


def _dot_nt(a, b):
    return lax.dot_general(a, b, (((1,), (1,)), ((), ())), preferred_element_type=jnp.float32)


def _rms_kernel(x_ref, g_ref, o_ref):
    x = x_ref[...]
    ms = jnp.mean(x * x, axis=-1, keepdims=True)
    o_ref[...] = (x * lax.rsqrt(ms + EPS) * g_ref[...]).astype(o_ref.dtype)


def _rmsnorm(x2d, g, out_dtype, tm=512):
    m, d = x2d.shape
    tm = min(tm, m)
    return pl.pallas_call(
        _rms_kernel,
        grid=(m // tm,),
        in_specs=[pl.BlockSpec((tm, d), lambda i: (i, 0)),
                  pl.BlockSpec((1, d), lambda i: (0, 0))],
        out_specs=pl.BlockSpec((tm, d), lambda i: (i, 0)),
        out_shape=jax.ShapeDtypeStruct((m, d), out_dtype),
        compiler_params=_params(1),
        name="rmsnorm",
    )(x2d, g.reshape(1, d))


def _cast_weight_blocks(w_refs, s_refs, wbf_scr):
    for t, (w_ref, s_ref) in enumerate(zip(w_refs, s_refs)):
        wbf_scr[:, t * W_BLOCK:(t + 1) * W_BLOCK] = (w_ref[...] * s_ref[...]).astype(wbf_scr.dtype)


def _proj_kernel(a_ref, *refs, nb):
    w_refs, s_refs, o_ref, wbf_scr = refs[:nb], refs[nb:2 * nb], refs[2 * nb], refs[2 * nb + 1]

    @pl.when(pl.program_id(1) == 0)
    def _():
        _cast_weight_blocks(w_refs, s_refs, wbf_scr)

    sub = min(MM_SUB, a_ref.shape[0])

    def body(s, carry):
        r0 = pl.multiple_of(s * sub, sub)
        acc = jnp.dot(a_ref[pl.ds(r0, sub), :], wbf_scr[...], preferred_element_type=jnp.float32)
        o_ref[pl.ds(r0, sub), :] = acc.astype(o_ref.dtype)
        return carry
    lax.fori_loop(0, a_ref.shape[0] // sub, body, 0, unroll=True)


def _weight_specs(layer, blocks, n_grid):
    def spec(t, weight):
        if n_grid == 2:
            imap = (lambda j, i: (layer, 0, blocks(j, t))) if weight else (lambda j, i: (0, blocks(j, t)))
        else:
            imap = (lambda i: (layer, 0, blocks(0, t))) if weight else (lambda i: (0, blocks(0, t)))
        return pl.BlockSpec((None, D_MODEL, W_BLOCK) if weight else (1, W_BLOCK), imap)
    return spec


def _project(a, w_in, scales, layer, blocks, nb, n_tiles, out_dtype):
    m, k = a.shape
    tm = min(MM_TM, m)
    tn = nb * W_BLOCK
    spec = _weight_specs(layer, blocks, 2)
    return pl.pallas_call(
        functools.partial(_proj_kernel, nb=nb),
        grid=(n_tiles, m // tm),
        in_specs=([pl.BlockSpec((tm, k), lambda j, i: (i, 0))]
                  + [spec(t, True) for t in range(nb)] + [spec(t, False) for t in range(nb)]),
        out_specs=pl.BlockSpec((tm, tn), lambda j, i: (i, j)),
        out_shape=jax.ShapeDtypeStruct((m, n_tiles * tn), out_dtype),
        scratch_shapes=[pltpu.VMEM((k, tn), jnp.bfloat16)],
        compiler_params=_params(2),
        name="in_proj",
    )(a, *([w_in] * nb), *([scales] * nb))


def _project_by_residue(a_ref, wbf_scr, o_ref, scrs, r):
    if r > STRIDE_STEP:
        scrs, tmp = scrs[:-1], scrs[-1]

    rows = a_ref.shape[0]
    n_slabs = wbf_scr.shape[1] // LANES
    sub = min(MM_SUB, rows)
    n_sub = rows // sub
    piece = sub // r

    def matmul_step(t):
        acc = jnp.dot(a_ref[t * sub:(t + 1) * sub, :], wbf_scr[...], preferred_element_type=jnp.float32)
        if r == 1:
            o_ref[0, t * sub:(t + 1) * sub, :] = acc.astype(o_ref.dtype)
        else:
            for k in range(n_slabs):
                scrs[t][k] = acc[:, k * LANES:(k + 1) * LANES]

    def relayout_step(t):
        out_rows = slice(t * piece, (t + 1) * piece)
        for k in range(n_slabs):
            cols = slice(k * LANES, (k + 1) * LANES)
            if r > STRIDE_STEP:
                r2 = r // STRIDE_STEP
                for c1 in range(STRIDE_STEP):
                    tmp[k, c1] = scrs[t][k, pl.ds(c1, sub // STRIDE_STEP, stride=STRIDE_STEP), :]
                for c1 in range(STRIDE_STEP):
                    for c2 in range(r2):
                        o_ref[c1 + STRIDE_STEP * c2, out_rows, cols] = (
                            tmp[k, c1, pl.ds(c2, piece, stride=r2), :].astype(o_ref.dtype))
            else:
                for c in range(r):
                    o_ref[c, out_rows, cols] = scrs[t][k, pl.ds(c, piece, stride=r), :].astype(o_ref.dtype)

    for t in range(n_sub + 1):
        if t < n_sub:
            matmul_step(t)
        if r > 1 and t > 0:
            relayout_step(t - 1)


def _residue_scratch(n, seq, r):
    sub = min(MM_SUB, seq)
    return (([pltpu.VMEM((n // LANES, sub, LANES), jnp.float32) for _ in range(seq // sub)] if r > 1 else [])
            + ([pltpu.VMEM((n // LANES, STRIDE_STEP, sub // STRIDE_STEP, LANES), jnp.float32)]
               if r > STRIDE_STEP else []))


def _cast_cols_kernel(w_ref, s_ref, o_ref):
    o_ref[...] = (w_ref[...] * s_ref[...]).astype(o_ref.dtype)


def _cast_cols(w_in, scales, layer, first_block, n_blocks):
    d = w_in.shape[1]
    return pl.pallas_call(
        _cast_cols_kernel,
        grid=(n_blocks,),
        in_specs=[pl.BlockSpec((None, d, W_BLOCK), lambda j: (layer, 0, first_block + j)),
                  pl.BlockSpec((1, W_BLOCK), lambda j: (0, first_block + j))],
        out_specs=pl.BlockSpec((d, W_BLOCK), lambda j: (0, j)),
        out_shape=jax.ShapeDtypeStruct((d, n_blocks * W_BLOCK), jnp.bfloat16),
        compiler_params=_params(1),
        name="cast_cols",
    )(w_in, scales)


def _mem_kv_kernel(mem_ref, g_ref, w_ref, o_ref):
    x = mem_ref[...]
    ms = jnp.mean(x * x, axis=-1, keepdims=True)
    mem_n = (x * lax.rsqrt(ms + EPS) * g_ref[...]).astype(jnp.bfloat16)
    o_ref[...] = jnp.dot(mem_n, w_ref[...].astype(jnp.bfloat16),
                         preferred_element_type=jnp.float32).astype(o_ref.dtype)


def _mem_kv(mem2d, g_mem, w_mem_kv):
    rows, d = mem2d.shape
    depth, _, n = w_mem_kv.shape
    return pl.pallas_call(
        _mem_kv_kernel,
        grid=(depth,),
        in_specs=[pl.BlockSpec((rows, d), lambda l: (0, 0)),
                  pl.BlockSpec((None, 1, d), lambda l: (l, 0, 0)),
                  pl.BlockSpec((None, d, n), lambda l: (l, 0, 0))],
        out_specs=pl.BlockSpec((None, rows, n), lambda l: (l, 0, 0)),
        out_shape=jax.ShapeDtypeStruct((depth, rows, n), jnp.bfloat16),
        compiler_params=_params(1),
        name="mem_kv",
    )(mem2d, g_mem.reshape(depth, 1, d), w_mem_kv)


def _t5_bucket(rel):
    half = REL_BUCKETS // 2
    max_exact = half // 2
    ret = jnp.where(rel > 0, half, 0)
    n = jnp.abs(rel)
    nf = jnp.maximum(n, 1).astype(jnp.float32)
    large = max_exact + (jnp.log(nf / max_exact) / math.log(REL_MAX_DIST / max_exact)
                         * (half - max_exact)).astype(jnp.int32)
    large = jnp.minimum(large, half - 1)
    return ret + jnp.where(n < max_exact, n, large)


def _toeplitz_kernel(u_ref, o_ref):
    n_tiles, rows, tile_w = o_ref.shape
    x = jnp.broadcast_to(u_ref[...], (rows, u_ref.shape[-1]))
    y = pltpu.roll(x, 0, 1, stride=1, stride_axis=0)
    for d in range(n_tiles):
        o_ref[d] = y[:, d * tile_w:(d + 1) * tile_w]


def _toeplitz_tiles(vals, rows, n_tiles, tile_w):
    groups = vals.shape[0]
    cols = n_tiles * tile_w
    period = pl.next_power_of_2(rows + cols - 1)
    pad = jnp.zeros((groups, period - (rows + cols - 1)), vals.dtype)
    u = jnp.concatenate([vals[:, rows - 1:], pad, vals[:, :rows - 1]], axis=1).reshape(groups, 1, period)
    return pl.pallas_call(
        _toeplitz_kernel,
        grid=(groups,),
        in_specs=[pl.BlockSpec((None, 1, period), lambda g: (g, 0, 0))],
        out_specs=pl.BlockSpec((None, n_tiles, rows, tile_w), lambda g: (g, 0, 0, 0)),
        out_shape=jax.ShapeDtypeStruct((groups, n_tiles, rows, tile_w), vals.dtype),
        compiler_params=_params(1),
        name="toeplitz_tiles",
    )(u)


def _diff_bias_tiles(rel_bias, seq):
    nd = seq // DIFF_TK - 1
    rel = jnp.arange(2 * seq - 1, dtype=jnp.int32) - (seq - 1)
    tvec = jnp.take(rel_bias[:, :DIFF_HEADS], _t5_bucket(rel), axis=0).T * LOG2E
    bias_max = jnp.broadcast_to(jnp.max(tvec, axis=1)[:, None, None], (DIFF_HEADS, 8, LANES))
    return _toeplitz_tiles(tvec, DIFF_TK, 2 * nd + 1, DIFF_TK), bias_max


def _dil_bias_tiles(rel_bias, g, seq):
    _, r = DIL_GROUPS[g]
    length = seq // r
    width = min(2 * DIL_TQ, length)
    shifts = jnp.array([0, -DIL_HALF, -(width - DIL_TQ)], dtype=jnp.int32)
    delta = shifts[:, None] + jnp.arange(DIL_TQ + width - 1, dtype=jnp.int32)[None, :] - (DIL_TQ - 1)
    c0 = DIFF_HEADS + g * DIL_HEADS
    bias = jnp.take(rel_bias[:, c0:c0 + DIL_HEADS], _t5_bucket(delta * r), axis=0) * LOG2E
    bias = jnp.where((jnp.abs(delta) <= DIL_HALF)[..., None], bias, NEG_INF)
    vals = jnp.transpose(bias, (0, 2, 1)).reshape(3 * DIL_HEADS, DIL_TQ + width - 1)
    return _toeplitz_tiles(vals, DIL_TQ, 1, width).reshape(3, DIL_HEADS, DIL_TQ, width)


def _diff_kernel(q0_ref, q1_ref, k0_ref, k1_ref, v_ref, bias_ref, bmax_ref, lam_ref, o_ref,
                 s_scr, mx_scr, vext_scr, acc_scr, kmax_scr, *, nkc):
    i = pl.program_id(2)
    tq = q0_ref.shape[0]
    nrb = tq // DIFF_TK

    @pl.when(i == 0)
    def _():
        for a in range(2):
            vext_scr[a, :, 0:LANES] = v_ref[:, a * LANES:(a + 1) * LANES]
            vext_scr[a, :, LANES:2 * LANES] = jnp.ones((vext_scr.shape[1], LANES), vext_scr.dtype)
        hlane = lax.broadcasted_iota(jnp.int32, (1, LANES), 1)
        for m, k_ref in enumerate((k0_ref, k1_ref)):
            col_max = jnp.max(jnp.square(k_ref[...].astype(jnp.float32)), axis=0, keepdims=True)
            for a in range(2):
                in_head = (hlane >= a * DIFF_QK_DIM) & (hlane < (a + 1) * DIFF_QK_DIM)
                kmax = jnp.sqrt(jnp.sum(jnp.where(in_head, col_max, 0.0), axis=-1, keepdims=True))
                kmax_scr[2 * a + m] = jnp.broadcast_to(kmax, kmax_scr.shape[1:])

    lp = lam_ref[...]
    lam_init = lp[4:5, 0:1]
    lam = (jnp.exp(jnp.sum(lp[0:1] * lp[1:2], axis=-1, keepdims=True))
           - jnp.exp(jnp.sum(lp[2:3] * lp[3:4], axis=-1, keepdims=True)) + lam_init)

    lane = lax.broadcasted_iota(jnp.int32, (tq, LANES), 1)
    q_refs = (q0_ref, q1_ref)
    k_refs = (k0_ref, k1_ref)

    def masked_q(u):
        a, m = divmod(u, 2)
        head_lanes = (lane >= a * DIFF_QK_DIM) & (lane < (a + 1) * DIFF_QK_DIM)
        q = q_refs[m][...]
        return jnp.where(head_lanes, q, jnp.zeros_like(q))

    def qk_chunk(u, qm, j):
        a, m = divmod(u, 2)
        s = _dot_nt(qm, k_refs[m][j * DIFF_TK:(j + 1) * DIFF_TK, :])
        for rb in range(nrb):
            rows = slice(rb * DIFF_TK, (rb + 1) * DIFF_TK)
            sb = s[rows] + bias_ref[a, j - nrb * i - rb + (nkc - 1)]
            s_scr[u % 2, j, rows, :] = sb
            mtile = jnp.maximum(sb[:, :LANES], sb[:, LANES:])
            if j == 0:
                mx_scr[u, rows, :] = mtile
            else:
                mx_scr[u, rows, :] = jnp.maximum(mx_scr[u, rows, :], mtile)

    def row_max(u):
        return jnp.broadcast_to(jnp.max(mx_scr[u], axis=-1, keepdims=True), (tq, DIFF_TK))

    def pv_chunk(u, mb, j, acc):
        e = jnp.exp2(s_scr[u % 2, j] - mb).astype(vext_scr.dtype)
        part = jnp.dot(e, vext_scr[u // 2, j * DIFF_TK:(j + 1) * DIFF_TK, :],
                       preferred_element_type=jnp.float32)
        return part if acc is None else acc + part

    def finish_head(a):
        n0 = acc_scr[2 * a]
        n1 = acc_scr[2 * a + 1]
        out = n0[:, :LANES] / n0[:, LANES:] - lam * (n1[:, :LANES] / n1[:, LANES:])
        ms = jnp.mean(out * out, axis=-1, keepdims=True)
        o_ref[:, a * LANES:(a + 1) * LANES] = (out * lax.rsqrt(ms + EPS) * (1.0 - lam_init)).astype(o_ref.dtype)

    n_units = 4

    def row_bound(u):
        a, m = divmod(u, 2)
        head_lanes = (lane >= a * DIFF_QK_DIM) & (lane < (a + 1) * DIFF_QK_DIM)
        qsq = jnp.square(q_refs[m][...].astype(jnp.float32))
        qn = jnp.sqrt(jnp.sum(jnp.where(head_lanes, qsq, 0.0), axis=-1, keepdims=True))
        ub = qn * (kmax_scr[u][0:1, 0:1] * DIFF_BOUND_SLACK) + bmax_ref[a][0:1, 0:1]
        return jnp.broadcast_to(ub, (tq, DIFF_TK))

    items = [(u, j) for u in range(n_units) for j in range(nkc)]
    qms, ubs, logits, accs = {}, {}, {}, {}
    min_sum = None
    for n in range(len(items) + DIFF_SKEW):
        if n < len(items):
            u, j = items[n]
            if j == 0:
                qms[u] = masked_q(u)
                ubs[u] = row_bound(u)
            a, m = divmod(u, 2)
            logits[n] = _dot_nt(qms[u], k_refs[m][j * DIFF_TK:(j + 1) * DIFF_TK, :])
        if n >= DIFF_SKEW:
            u, j = items[n - DIFF_SKEW]
            a = u // 2
            s = logits.pop(n - DIFF_SKEW)
            e = jnp.concatenate(
                [jnp.exp2(s[rb * DIFF_TK:(rb + 1) * DIFF_TK] + bias_ref[a, j - nrb * i - rb + (nkc - 1)]
                          - ubs[u][rb * DIFF_TK:(rb + 1) * DIFF_TK]) for rb in range(nrb)],
                axis=0).astype(vext_scr.dtype)
            part = jnp.dot(e, vext_scr[a, j * DIFF_TK:(j + 1) * DIFF_TK, :], preferred_element_type=jnp.float32)
            accs[u] = part if j == 0 else accs[u] + part
            if j == nkc - 1:
                acc = accs.pop(u)
                acc_scr[u] = acc
                row_sum_min = jnp.min(acc[:, LANES:])
                min_sum = row_sum_min if min_sum is None else jnp.minimum(min_sum, row_sum_min)
                if u % 2 == 1:
                    finish_head(u // 2)

    @pl.when(min_sum < 2.0 ** DIFF_MIN_LOG2_SUM)
    def _():
        mb_prev = None
        for u in range(n_units + 1):
            qm = masked_q(u) if u < n_units else None
            acc = None
            for j in range(nkc):
                if u < n_units:
                    qk_chunk(u, qm, j)
                if u > 0:
                    acc = pv_chunk(u - 1, mb_prev, j, acc)
            if u > 0:
                acc_scr[u - 1] = acc
                if (u - 1) % 2 == 1:
                    finish_head((u - 1) // 2)
            if u < n_units:
                mb_prev = row_max(u)


def _diff_attention(zd, bias_tiles, bias_max, lam_p, lam_init):
    b, s, _ = zd.shape
    tq = DIFF_TQ
    nkc = s // DIFF_TK
    kern = functools.partial(_diff_kernel, nkc=nkc)
    lam_in = jnp.concatenate([lam_p, jnp.full((1, DIFF_QK_DIM), lam_init, lam_p.dtype)], axis=0)
    kblk = OFF_DK // LANES
    vblk = OFF_DV // (2 * LANES)
    return pl.pallas_call(
        kern,
        grid=(DIFF_HEADS // 2, b, s // tq),
        in_specs=[
            pl.BlockSpec((None, tq, LANES), lambda hp, bb, i: (bb, i, hp)),
            pl.BlockSpec((None, tq, LANES), lambda hp, bb, i: (bb, i, DIFF_HEADS // 2 + hp)),
            pl.BlockSpec((None, s, LANES), lambda hp, bb, i: (bb, 0, kblk + hp)),
            pl.BlockSpec((None, s, LANES), lambda hp, bb, i: (bb, 0, kblk + DIFF_HEADS // 2 + hp)),
            pl.BlockSpec((None, s, 2 * LANES), lambda hp, bb, i: (bb, 0, vblk + hp)),
            pl.BlockSpec((2, 2 * nkc - 1, DIFF_TK, DIFF_TK), lambda hp, bb, i: (hp, 0, 0, 0)),
            pl.BlockSpec((2, 8, LANES), lambda hp, bb, i: (hp, 0, 0)),
            pl.BlockSpec((5, DIFF_QK_DIM), lambda hp, bb, i: (0, 0)),
        ],
        out_specs=pl.BlockSpec((None, tq, 2 * LANES), lambda hp, bb, i: (bb, i, hp)),
        out_shape=jax.ShapeDtypeStruct((b, s, DIFF_HEADS * DIFF_V_DIM), jnp.bfloat16),
        scratch_shapes=[
            pltpu.VMEM((2, nkc, tq, DIFF_TK), jnp.float32),
            pltpu.VMEM((4, tq, LANES), jnp.float32),
            pltpu.VMEM((2, s, 2 * LANES), jnp.bfloat16),
            pltpu.VMEM((4, tq, 2 * LANES), jnp.float32),
            pltpu.VMEM((4, 8, LANES), jnp.float32),
        ],
        compiler_params=_params(3),
        name="diff_attn",
    )(zd, zd, zd, zd, zd, bias_tiles, bias_max, lam_in)


def _dil_kernel(a_ref, *refs, nb, r, length, width):
    w_refs, s_refs = refs[:nb], refs[nb:2 * nb]
    bias_ref, o_ref, lse_ref, wbf_scr, z_ref = refs[2 * nb:2 * nb + 5]

    @pl.when(pl.program_id(0) == 0)
    def _():
        _cast_weight_blocks(w_refs, s_refs, wbf_scr)

    _project_by_residue(a_ref, wbf_scr, z_ref, refs[2 * nb + 5:], r)

    nqb = length // DIL_TQ
    lane = lax.broadcasted_iota(jnp.int32, (DIL_TQ, LANES), 1)

    def block_coords(t):
        if nqb == 1:
            return t, 0
        if r == 1:
            return 0, t
        return t // nqb, t % nqb

    def body(tt, carry):
        staged = []
        for i in range(DIL_UNROLL):
            c, qb = block_coords(tt * DIL_UNROLL + i)
            q0 = pl.multiple_of(qb * DIL_TQ, DIL_TQ)
            ws = pl.multiple_of(jnp.clip(q0 - DIL_HALF, 0, length - width), DIL_HALF)
            var = jnp.where(qb == 0, 0, jnp.where(qb == nqb - 1, 2, 1))
            scs = []
            for h in range(DIL_HEADS):
                col = h * HEAD_DIM
                q = z_ref[c, pl.ds(q0, DIL_TQ), col:col + HEAD_DIM]
                kw = z_ref[c, pl.ds(ws, width), col + DIL_WIDTH:col + DIL_WIDTH + HEAD_DIM]
                scs.append(_dot_nt(q, kw) + bias_ref[var, h])
            staged.append((c, q0, ws, scs))
        for c, q0, ws, scs in staged:
            out_rows = pl.ds(q0 * r + c, DIL_TQ, stride=r) if r > 1 else pl.ds(q0, DIL_TQ)
            lse_tile = jnp.zeros((DIL_TQ, LANES), jnp.float32)
            for h in range(DIL_HEADS):
                col = 2 * DIL_WIDTH + h * HEAD_DIM
                vw = z_ref[c, pl.ds(ws, width), col:col + HEAD_DIM]
                sc = scs[h]
                mrow = jnp.max(sc, axis=-1, keepdims=True)
                e = jnp.exp2(sc - mrow)
                den = jnp.sum(e, axis=-1, keepdims=True)
                o_ref[h, out_rows, :] = jnp.dot(e.astype(vw.dtype), vw,
                                                preferred_element_type=jnp.float32) / den
                lse_tile = jnp.where(lane == h, mrow + jnp.log2(den), lse_tile)
            lse_ref[out_rows, :] = lse_tile
        return carry
    lax.fori_loop(0, r * nqb // DIL_UNROLL, body, 0)


def _dil_attention(h, w_in, scales, layer, g, bias_tiles, batch):
    m, k = h.shape
    s = m // batch
    _, r = DIL_GROUPS[g]
    length = s // r
    nb = 3
    n = nb * W_BLOCK
    width = bias_tiles.shape[-1]
    spec = _weight_specs(layer, lambda j, t: (OFF_LQ, OFF_LK, OFF_LV)[t] // W_BLOCK + g, 1)
    kern = functools.partial(_dil_kernel, nb=nb, r=r, length=length, width=width)
    return pl.pallas_call(
        kern,
        grid=(batch,),
        in_specs=([pl.BlockSpec((s, k), lambda bb: (bb, 0))]
                  + [spec(t, True) for t in range(nb)] + [spec(t, False) for t in range(nb)]
                  + [pl.BlockSpec(bias_tiles.shape, lambda bb: (0, 0, 0, 0))]),
        out_specs=[pl.BlockSpec((None, DIL_HEADS, s, HEAD_DIM), lambda bb: (bb, 0, 0, 0)),
                   pl.BlockSpec((None, s, LANES), lambda bb: (bb, 0, 0))],
        out_shape=[jax.ShapeDtypeStruct((batch, DIL_HEADS, s, HEAD_DIM), jnp.float32),
                   jax.ShapeDtypeStruct((batch, s, LANES), jnp.float32)],
        scratch_shapes=([pltpu.VMEM((k, n), jnp.bfloat16), pltpu.VMEM((r, length, n), jnp.bfloat16)]
                        + _residue_scratch(n, s, r)),
        compiler_params=_params(1),
        name=f"dil_attn_g{g}",
    )(h, *([w_in] * nb), *([scales] * nb), bias_tiles)


def _finish_kernel(x_ref, oa_ref, ob0_ref, ob1_ref, ob2_ref, l0_ref, l1_ref, l2_ref, dg_ref, h_ref, wg_ref,
                   mkv_ref, wa_ref, wb_ref, wm_ref, wo_ref, g_ref, *refs, final):
    out_refs, zg_ref = refs[:-1], refs[-1]
    bf16 = jnp.bfloat16
    f32 = jnp.float32

    pending = list(range(wg_ref.shape[1] // W_BLOCK))

    def project(n):
        for _ in range(n):
            cb = pending.pop(0)
            cols = slice(cb * W_BLOCK, (cb + 1) * W_BLOCK)
            zg_ref[:, cols] = jnp.dot(h_ref[...], wg_ref[:, cols], preferred_element_type=f32).astype(zg_ref.dtype)

    def half_silu(zh):
        zh = zh.astype(f32)
        return zh + zh * jnp.tanh(zh)

    c_lg, c_mq, c_mg, c_gate = 0, DIL_WIDTH, DIL_WIDTH + MEM_WIDTH, DIL_WIDTH + 2 * MEM_WIDTH

    project(3)

    ya = jnp.dot((oa_ref[...].astype(f32) * half_silu(dg_ref[...])).astype(bf16), wa_ref[...],
                 preferred_element_type=jnp.float32)
    project(2)

    l0, l1, l2 = l0_ref[...], l1_ref[...], l2_ref[...]
    lmax = jnp.maximum(jnp.maximum(l0, l1), l2)
    w0, w1, w2 = jnp.exp2(l0 - lmax), jnp.exp2(l1 - lmax), jnp.exp2(l2 - lmax)
    inv = 1.0 / (w0 + w1 + w2)
    w0, w1, w2 = w0 * inv, w1 * inv, w2 * inv
    ob_refs = (ob0_ref, ob1_ref, ob2_ref)
    parts = []
    for h in range(DIL_HEADS):
        acc = None
        for wg, ob_ref in zip((w0, w1, w2), ob_refs):
            term = wg[:, h:h + 1] * ob_ref[h]
            acc = term if acc is None else acc + term
        parts.append(acc)
    ob = jnp.concatenate(parts, axis=-1)
    yb = jnp.dot((ob * half_silu(zg_ref[:, c_lg:c_lg + DIL_WIDTH])).astype(bf16), wb_ref[...],
                 preferred_element_type=jnp.float32)
    project(2)

    scs = [_dot_nt(zg_ref[:, c_mq + h * HEAD_DIM:c_mq + (h + 1) * HEAD_DIM],
                   mkv_ref[:, h * HEAD_DIM:(h + 1) * HEAD_DIM]) for h in range(MEM_HEADS)]
    project(1)
    parts = []
    for h in range(MEM_HEADS):
        sc = scs[h]
        e = jnp.exp2(sc - jnp.max(sc, axis=-1, keepdims=True))
        den = jnp.sum(e, axis=-1, keepdims=True)
        vcols = slice(MEM_WIDTH + h * HEAD_DIM, MEM_WIDTH + (h + 1) * HEAD_DIM)
        parts.append(jnp.dot(e.astype(bf16), mkv_ref[:, vcols], preferred_element_type=jnp.float32) / den)
    om = jnp.concatenate(parts, axis=-1)
    project(1)
    ym = jnp.dot((om * half_silu(zg_ref[:, c_mg:c_mg + MEM_WIDTH])).astype(bf16), wm_ref[...],
                 preferred_element_type=jnp.float32)
    project(len(pending))

    merged = (ya + yb + ym
              + jnp.tanh(zg_ref[:, c_gate:c_gate + D_MODEL].astype(f32)) * ya
              + jnp.tanh(zg_ref[:, c_gate + D_MODEL:c_gate + 2 * D_MODEL].astype(f32)) * yb
              + jnp.tanh(zg_ref[:, c_gate + 2 * D_MODEL:c_gate + 3 * D_MODEL].astype(f32)) * ym)
    xn = x_ref[...] + jnp.dot(merged.astype(bf16), wo_ref[...], preferred_element_type=jnp.float32)
    hn = xn * lax.rsqrt(jnp.mean(xn * xn, axis=-1, keepdims=True) + EPS) * g_ref[...]
    if final:
        out_refs[0][...] = hn
    else:
        out_refs[0][...] = xn
        out_refs[1][...] = hn.astype(out_refs[1].dtype)


def _finish(x, oa, obs, lses, zd, h, wg, mkv, wa, wb, wm, wo, g_next, final):
    b, s, d = x.shape
    t = FIN_T
    row = lambda width: pl.BlockSpec((None, t, width), lambda bb, i: (bb, i, 0))
    full = lambda arr: pl.BlockSpec(arr.shape, lambda bb, i: (0,) * arr.ndim, pipeline_mode=pl.Buffered(1))
    heads = pl.BlockSpec((None, DIL_HEADS, t, HEAD_DIM), lambda bb, i: (bb, 0, i, 0))
    in_specs = [row(d), row(d), heads, heads, heads,
                row(LANES), row(LANES), row(LANES),
                pl.BlockSpec((None, t, d), lambda bb, i: (bb, i, OFF_DG // D_MODEL)),
                row(d), full(wg),
                pl.BlockSpec((None, N_MEM, 2 * MEM_WIDTH), lambda bb, i: (bb, 0, 0)),
                full(wa), full(wb), full(wm), full(wo),
                pl.BlockSpec((1, d), lambda bb, i: (0, 0))]
    if final:
        out_specs = [row(d)]
        out_shape = [jax.ShapeDtypeStruct((b, s, d), jnp.float32)]
    else:
        out_specs = [row(d), row(d)]
        out_shape = [jax.ShapeDtypeStruct((b, s, d), jnp.float32),
                     jax.ShapeDtypeStruct((b, s, d), jnp.bfloat16)]
    return pl.pallas_call(
        functools.partial(_finish_kernel, final=final),
        grid=(b, s // t),
        in_specs=in_specs,
        out_specs=out_specs,
        out_shape=out_shape,
        scratch_shapes=[pltpu.VMEM((t, wg.shape[1]), jnp.bfloat16)],
        compiler_params=_params(2),
        name="finish",
    )(x, oa, *obs, *lses, zd, h, wg, mkv, wa, wb, wm, wo, g_next.reshape(1, d))


def kernel(x, mem, g_norm, w_in, diff_lambda, w_mem_kv, g_mem, w_br_diff, w_br_dil, w_br_mem, w_out,
           rel_bias, g_final):
    b, s, d = x.shape
    depth = w_in.shape[0]
    bf16 = jnp.bfloat16
    m_rows = b * s

    diff_tiles, diff_bias_max = _diff_bias_tiles(rel_bias, s)
    dil_tiles = [_dil_bias_tiles(rel_bias, g, s) for g in range(len(DIL_GROUPS))]
    mkv_all = _mem_kv(mem.reshape(b * N_MEM, d), g_mem, w_mem_kv)

    col_scale = np.ones((1, N_IN), np.float32)
    col_scale[0, OFF_DQ:OFF_DK] = DIFF_QK_DIM ** -0.5 * LOG2E
    col_scale[0, OFF_LQ:OFF_LK] = HEAD_DIM ** -0.5 * LOG2E
    col_scale[0, OFF_MQ:OFF_MG] = HEAD_DIM ** -0.5 * LOG2E
    col_scale[0, OFF_DG:OFF_LQ] = 0.5
    col_scale[0, OFF_LG:OFF_MQ] = 0.5
    col_scale[0, OFF_MG:] = 0.5
    col_scale = jnp.asarray(col_scale)

    h = _rmsnorm(x.reshape(m_rows, d), g_norm[0], bf16)
    out = None
    for l in range(depth):
        zd = _project(h, w_in, col_scale, l, lambda j, t: 4 * j + t, 4, OFF_LQ // (4 * W_BLOCK),
                      bf16).reshape(b, s, -1)
        wg = _cast_cols(w_in, col_scale, l, OFF_LG // W_BLOCK, (N_IN - OFF_LG) // W_BLOCK)
        mkv = mkv_all[l].reshape(b, N_MEM, 2 * MEM_WIDTH)
        lam_init = 0.8 - 0.6 * math.exp(-0.3 * l)
        oa = _diff_attention(zd, diff_tiles, diff_bias_max, diff_lambda[l], lam_init)
        obs, lses = zip(*[_dil_attention(h, w_in, col_scale, l, g, dil_tiles[g], b)
                          for g in range(len(DIL_GROUPS))])

        final = l == depth - 1
        g_next = g_final if final else g_norm[l + 1]
        res = _finish(x, oa, obs, lses, zd, h.reshape(b, s, d), wg, mkv,
                      w_br_diff[l].astype(bf16), w_br_dil[l].astype(bf16),
                      w_br_mem[l].astype(bf16), (0.5 * w_out[l]).astype(bf16), g_next, final)
        if final:
            out = res[0]
        else:
            x, h3 = res
            h = h3.reshape(m_rows, d)
    return out
```
